```python
import math
import jax
import jax.numpy as jnp
from jax import lax
import numpy as np

D_MODEL = 2048
BATCH = 4
SEQ = 4096
DEPTH = 2

CTX_LEN = 256
GRID_W = 64
CONV_W = 1024
CONV_K = 31
DA_HEADS = 8
DA_DH = 64
DA_DV = 2 * DA_DH
DA_W = DA_HEADS * DA_DV
ROPE_BASE = 10000.0
Q_BLOCK = 128
LRU_W = 1024
LRU_BLOCKS = 16
LRU_BS = LRU_W // LRU_BLOCKS
LRU_CONV_K = 4
LRU_C = 8.0
N_BRANCH = 3
N_EXPERTS = 32
TOP_K = 4
D_EXPERT = 1024
SWIGLU_LIMIT = 7.0
SWIGLU_ALPHA = 1.702
MOE_BLOCK = 128
LN_EPS = 1e-5
DEEPNORM_ALPHA = (2 * DEPTH) ** 0.25
DEEPNORM_BETA = (8 * DEPTH) ** -0.25
IN_SPLIT = (2 * CONV_W, DA_W, DA_W, DA_W, LRU_W, LRU_W, N_BRANCH * D_MODEL)
IN_W = sum(IN_SPLIT)

kernel_name = "hybrid_conv_diffattn_rglru_moe_dit"


def layer_norm(x, g=None, b=None):
    xf = x.astype(jnp.float32)
    mu = jnp.mean(xf, axis=-1, keepdims=True)
    var = jnp.mean(jnp.square(xf - mu), axis=-1, keepdims=True)
    y = ((xf - mu) * lax.rsqrt(var + LN_EPS)).astype(x.dtype)
    return y if g is None else y * g + b


def rms_norm(x, g):
    xf = x.astype(jnp.float32)
    y = xf * lax.rsqrt(jnp.mean(xf * xf, axis=-1, keepdims=True) + LN_EPS)
    return y.astype(x.dtype) * g


def modulate(x, shift, scale):
    return layer_norm(x) * (1 + scale) + shift


def split_cols(z):
    outs, start = [], 0
    for w in IN_SPLIT:
        outs.append(z[..., start:start + w])
        start += w
    return outs


def depthwise_conv(x, w, pad_lo, pad_hi):
    return lax.conv_general_dilated(
        x, w[:, None, :], window_strides=(1,), padding=((pad_lo, pad_hi),),
        dimension_numbers=("NWC", "WIO", "NWC"), feature_group_count=x.shape[-1])


def conformer_conv_branch(z, dw, db, ln_g, ln_b, w_o):
    h = z[..., :CONV_W] * jax.nn.sigmoid(z[..., CONV_W:])
    h = depthwise_conv(h, dw, CONV_K // 2, CONV_K // 2) + db
    h = jax.nn.silu(layer_norm(h, ln_g, ln_b))
    return h @ w_o


def axial_rope(n_tok):
    rows = n_tok // GRID_W
    row = jnp.repeat(jnp.arange(rows, dtype=jnp.float32), GRID_W)
    col = jnp.tile(jnp.arange(GRID_W, dtype=jnp.float32), rows)
    n_freq = DA_DH // 4
    inv_freq = ROPE_BASE ** (-jnp.arange(n_freq, dtype=jnp.float32) / n_freq)
    ang = jnp.concatenate([row[:, None] * inv_freq, col[:, None] * inv_freq], axis=-1)
    return jnp.cos(ang), jnp.sin(ang)


def apply_rope(x, cos, sin):
    cos = cos[None, :, None, None, :].astype(x.dtype)
    sin = sin[None, :, None, None, :].astype(x.dtype)
    x1, x2 = x[..., :DA_DH // 2], x[..., DA_DH // 2:]
    return jnp.concatenate([x1 * cos - x2 * sin, x1 * sin + x2 * cos], axis=-1)


def diff_attend(q1, q2, k1, k2, v, lam):
    s1 = jnp.einsum("bqhd,bkhd->bhqk", q1, k1).astype(jnp.float32) * (DA_DH ** -0.5)
    s2 = jnp.einsum("bqhd,bkhd->bhqk", q2, k2).astype(jnp.float32) * (DA_DH ** -0.5)
    p = jax.nn.softmax(s1, axis=-1) - lam * jax.nn.softmax(s2, axis=-1)
    return jnp.einsum("bhqk,bkhd->bqhd", p.astype(v.dtype), v)


def diff_attend_blocked(q1, q2, k1, k2, v, lam):
    B, L, H, _ = q1.shape
    nb = L // Q_BLOCK
    to_blocks = lambda q: q.reshape(B, nb, Q_BLOCK, H, DA_DH).swapaxes(0, 1)
    out = lax.map(lambda qs: diff_attend(qs[0], qs[1], k1, k2, v, lam), (to_blocks(q1), to_blocks(q2)))
    return out.swapaxes(0, 1).reshape(B, L, H, DA_DV)


def diff_attention_branch(zq_x, zk_x, zv_x, zq_c, zk_c, zv_c, lam_p, norm_g, w_o, layer_idx, ctx_out):
    lam_init = 0.8 - 0.6 * math.exp(-0.3 * layer_idx)
    lp = lam_p.astype(jnp.float32)
    lam = jnp.exp(jnp.sum(lp[0] * lp[1])) - jnp.exp(jnp.sum(lp[2] * lp[3])) + lam_init
    heads_qk = lambda z: z.reshape(*z.shape[:-1], DA_HEADS, 2, DA_DH)
    heads_v = lambda z: z.reshape(*z.shape[:-1], DA_HEADS, DA_DV)

    def head_out(o):
        o = rms_norm(o, norm_g) * (1.0 - lam_init)
        return o.reshape(*o.shape[:-2], DA_W) @ w_o

    cos, sin = axial_rope(zq_x.shape[1])
    q_x = apply_rope(heads_qk(zq_x), cos, sin)
    k_x = apply_rope(heads_qk(zk_x), cos, sin)
    k_c, v_c = heads_qk(zk_c), heads_v(zv_c)
    k_all = jnp.concatenate([k_c, k_x], axis=1)
    v_all = jnp.concatenate([v_c, heads_v(zv_x)], axis=1)
    o_x = diff_attend_blocked(q_x[..., 0, :], q_x[..., 1, :], k_all[..., 0, :], k_all[..., 1, :], v_all, lam)
    out_c = None
    if ctx_out:
        q_c = heads_qk(zq_c)
        out_c = head_out(diff_attend(q_c[..., 0, :], q_c[..., 1, :], k_c[..., 0, :], k_c[..., 1, :], v_c, lam))
    return head_out(o_x), out_c


def block_diag(x, w):
    xb = x.reshape(*x.shape[:-1], LRU_BLOCKS, LRU_BS)
    return jnp.einsum("blnj,njk->blnk", xb, w).reshape(x.shape)


def rglru_coeffs(xs, gate_w, gate_b, lam):
    r = jax.nn.sigmoid(block_diag(xs, gate_w[0]) + gate_b[0])
    i = jax.nn.sigmoid(block_diag(xs, gate_w[1]) + gate_b[1])
    log_a = -LRU_C * r * jax.nn.softplus(-lam)
    a = jnp.exp(log_a)
    b = jnp.sqrt(-jnp.expm1(2.0 * log_a)) * (i * xs)
    return a, b


def linear_scan(a, b, h0, reverse):
    def combine(e1, e2):
        return e1[0] * e2[0], e2[0] * e1[1] + e2[1]
    a_cum, h = lax.associative_scan(combine, (a, b), reverse=reverse, axis=1)
    return h if h0 is None else h + a_cum * h0[:, None, :]


def rglru_branch(zx_x, zg_x, zx_c, zg_c, conv_w, conv_b, gate_w, gate_b, lam, w_o, ctx_out):
    xs_x = (depthwise_conv(zx_x, conv_w, 2, 1) + conv_b).astype(jnp.float32)
    xs_c = (depthwise_conv(zx_c, conv_w, 2, 1) + conv_b).astype(jnp.float32)
    h_x, h_c = [], []
    for d, rev in enumerate((False, True)):
        gw, gb, ld = gate_w[d].astype(jnp.float32), gate_b[d].astype(jnp.float32), lam[d].astype(jnp.float32)
        a_c, b_c = rglru_coeffs(xs_c, gw, gb, ld)
        hc = linear_scan(a_c, b_c, None, rev)
        h_end = hc[:, 0] if rev else hc[:, -1]
        a_x, b_x = rglru_coeffs(xs_x, gw, gb, ld)
        h_x.append(linear_scan(a_x, b_x, h_end, rev))
        h_c.append(hc)
    out_x = ((h_x[0] + h_x[1]).astype(zg_x.dtype) * jax.nn.gelu(zg_x)) @ w_o
    out_c = None
    if ctx_out:
        out_c = ((h_c[0] + h_c[1]).astype(zg_c.dtype) * jax.nn.gelu(zg_c)) @ w_o
    return out_x, out_c


def merge_branches(zg, br_conv, br_att, br_lru):
    g = jax.nn.sigmoid(zg).reshape(*zg.shape[:-1], N_BRANCH, D_MODEL)
    return g[..., 0, :] * br_conv + g[..., 1, :] * br_att + g[..., 2, :] * br_lru


def moe_ffn(t, router_w, router_b, w1, b1, w2, b2):
    T = t.shape[0]
    logits = (t @ router_w + router_b).astype(jnp.float32)
    top_val, top_idx = lax.top_k(logits, TOP_K)
    top_w = jax.nn.softmax(top_val, axis=-1)
    n_assign = T * TOP_K
    n_blocks = -(-n_assign // MOE_BLOCK) + N_EXPERTS
    n_slots = n_blocks * MOE_BLOCK
    e_flat = top_idx.reshape(-1)
    tok_flat = jnp.arange(n_assign, dtype=jnp.int32) // TOP_K
    order = jnp.argsort(e_flat)
    e_sorted = e_flat[order]
    counts = jnp.zeros((N_EXPERTS,), jnp.int32).at[e_flat].add(1)
    padded = (counts + MOE_BLOCK - 1) // MOE_BLOCK * MOE_BLOCK
    start = jnp.cumsum(counts) - counts
    pad_end = jnp.cumsum(padded)
    pad_start = pad_end - padded
    dest = pad_start[e_sorted] + jnp.arange(n_assign, dtype=jnp.int32) - start[e_sorted]
    slot_tok = jnp.zeros((n_slots,), jnp.int32).at[dest].set(tok_flat[order])
    slot_w = jnp.zeros((n_slots,), jnp.float32).at[dest].set(top_w.reshape(-1)[order])
    blk_start = jnp.arange(n_blocks, dtype=jnp.int32) * MOE_BLOCK
    blk_e = jnp.minimum(jnp.searchsorted(pad_end, blk_start, side="right"), N_EXPERTS - 1)

    def expert_block(args):
        tok, e = args
        h = t[tok] @ w1[e] + b1[e]
        gl = jnp.minimum(h[:, 0::2], SWIGLU_LIMIT)
        lin = jnp.clip(h[:, 1::2], -SWIGLU_LIMIT, SWIGLU_LIMIT)
        act = gl * jax.nn.sigmoid(SWIGLU_ALPHA * gl) * (lin + 1)
        return act @ w2[e] + b2[e]

    outs = lax.map(expert_block, (slot_tok.reshape(n_blocks, MOE_BLOCK), blk_e))
    y = jnp.zeros(t.shape, jnp.float32).at[slot_tok].add(outs.reshape(n_slots, -1) * slot_w[:, None])
    return y.astype(t.dtype)


def setup_inputs(seed: int = 0) -> dict:
    key = jax.random.key(seed)
    ks = jax.random.split(key, 40)
    L, D, E, F = DEPTH, D_MODEL, N_EXPERTS, D_EXPERT
    nrm = lambda k, shape, s: jax.random.normal(k, shape, jnp.float32) * s
    a0 = jax.random.uniform(ks[20], (L, 2, LRU_W), jnp.float32, minval=0.9, maxval=0.999)
    return {
        "x": nrm(ks[0], (BATCH, SEQ, D), 1.0),
        "c": nrm(ks[1], (BATCH, D), 1.0),
        "ctx": nrm(ks[2], (BATCH, CTX_LEN, D), 1.0),
        "c_ctx": nrm(ks[3], (D,), 1.0),
        "w_ada": nrm(ks[4], (L, D, 6 * D), D ** -0.5),
        "b_ada": nrm(ks[5], (L, 6 * D), 0.01),
        "w_in": nrm(ks[6], (L, D, IN_W), D ** -0.5),
        "b_in": nrm(ks[7], (L, IN_W), 0.01),
        "conv_dw": nrm(ks[8], (L, CONV_K, CONV_W), CONV_K ** -0.5),
        "conv_db": nrm(ks[9], (L, CONV_W), 0.01),
        "conv_ln_g": 1.0 + nrm(ks[10], (L, CONV_W), 0.01),
        "conv_ln_b": nrm(ks[11], (L, CONV_W), 0.01),
        "w_conv_out": nrm(ks[12], (L, CONV_W, D), CONV_W ** -0.5 * DEEPNORM_BETA),
        "da_lambda": nrm(ks[13], (L, 4, DA_DH), 0.1),
        "da_norm_g": 1.0 + nrm(ks[14], (L, DA_DV), 0.01),
        "w_da_out": nrm(ks[15], (L, DA_W, D), DA_W ** -0.5 * DEEPNORM_BETA),
        "lru_conv_w": nrm(ks[16], (L, LRU_CONV_K, LRU_W), LRU_CONV_K ** -0.5),
        "lru_conv_b": nrm(ks[17], (L, LRU_W), 0.01),
        "lru_gate_w": nrm(ks[18], (L, 2, 2, LRU_BLOCKS, LRU_BS, LRU_BS), LRU_BS ** -0.5),
        "lru_gate_b": nrm(ks[19], (L, 2, 2, LRU_W), 0.01),
        "lru_lambda": jnp.log(a0) - jnp.log1p(-a0),
        "w_lru_out": nrm(ks[21], (L, LRU_W, D), LRU_W ** -0.5 * DEEPNORM_BETA),
        "w_out": nrm(ks[22], (L, D, D), D ** -0.5 * DEEPNORM_BETA),
        "ln1_g": 1.0 + nrm(ks[23], (L, D), 0.01),
        "ln1_b": nrm(ks[24], (L, D), 0.01),
        "router_w": nrm(ks[25], (L, D, E), D ** -0.5),
        "router_b": nrm(ks[26], (L, E), 0.01),
        "moe_w1": nrm(ks[27], (L, E, D, 2 * F), D ** -0.5),
        "moe_b1": nrm(ks[28], (L, E, 2 * F), 0.01),
        "moe_w2": nrm(ks[29], (L, E, F, D), F ** -0.5 * DEEPNORM_BETA),
        "moe_b2": nrm(ks[30], (L, E, D), 0.01),
        "ln2_g": 1.0 + nrm(ks[31], (L, D), 0.01),
        "ln2_b": nrm(ks[32], (L, D), 0.01),
    }


def reference(x, c, ctx, c_ctx, w_ada, b_ada, w_in, b_in, conv_dw, conv_db, conv_ln_g, conv_ln_b,
              w_conv_out, da_lambda, da_norm_g, w_da_out, lru_conv_w, lru_conv_b, lru_gate_w,
              lru_gate_b, lru_lambda, w_lru_out, w_out, ln1_g, ln1_b, router_w, router_b,
              moe_w1, moe_b1, moe_w2, moe_b2, ln2_g, ln2_b):
    h_ctx = ctx
    for l in range(DEPTH):
        ctx_out = l < DEPTH - 1
        ada_x = jnp.split(jax.nn.silu(c) @ w_ada[l] + b_ada[l], 6, axis=-1)
        ada_c = jnp.split(jax.nn.silu(c_ctx) @ w_ada[l] + b_ada[l], 6, axis=-1)
        sh1, sc1, g1, sh2, sc2, g2 = [m[:, None, :] for m in ada_x]
        csh1, csc1, cg1, csh2, csc2, cg2 = ada_c

        zx = split_cols(modulate(x, sh1, sc1) @ w_in[l] + b_in[l])
        zc = split_cols(modulate(h_ctx, csh1, csc1) @ w_in[l] + b_in[l])
        conv_x = conformer_conv_branch(zx[0], conv_dw[l], conv_db[l], conv_ln_g[l], conv_ln_b[l], w_conv_out[l])
        att_x, att_c = diff_attention_branch(zx[1], zx[2], zx[3], zc[1], zc[2], zc[3],
                                             da_lambda[l], da_norm_g[l], w_da_out[l], l, ctx_out)
        lru_x, lru_c = rglru_branch(zx[4], zx[5], zc[4], zc[5], lru_conv_w[l], lru_conv_b[l],
                                    lru_gate_w[l], lru_gate_b[l], lru_lambda[l], w_lru_out[l], ctx_out)
        mix_x = merge_branches(zx[6], conv_x, att_x, lru_x) @ w_out[l]
        x = layer_norm(DEEPNORM_ALPHA * x + g1 * mix_x, ln1_g[l], ln1_b[l])
        u_x = modulate(x, sh2, sc2)

        if ctx_out:
            conv_c = conformer_conv_branch(zc[0], conv_dw[l], conv_db[l], conv_ln_g[l], conv_ln_b[l], w_conv_out[l])
            mix_c = merge_branches(zc[6], conv_c, att_c, lru_c) @ w_out[l]
            h_ctx = layer_norm(DEEPNORM_ALPHA * h_ctx + cg1 * mix_c, ln1_g[l], ln1_b[l])
            u_c = modulate(h_ctx, csh2, csc2)
            n_c = u_c.shape[0] * u_c.shape[1]
            tokens = jnp.concatenate([u_c.reshape(-1, D_MODEL), u_x.reshape(-1, D_MODEL)], axis=0)
            y = moe_ffn(tokens, router_w[l], router_b[l], moe_w1[l], moe_b1[l], moe_w2[l], moe_b2[l])
            h_ctx = layer_norm(DEEPNORM_ALPHA * h_ctx + cg2 * y[:n_c].reshape(h_ctx.shape), ln2_g[l], ln2_b[l])
            ffn_x = y[n_c:].reshape(x.shape)
        else:
            ffn_x = moe_ffn(u_x.reshape(-1, D_MODEL), router_w[l], router_b[l],
                            moe_w1[l], moe_b1[l], moe_w2[l], moe_b2[l]).reshape(x.shape)
        x = layer_norm(DEEPNORM_ALPHA * x + g2 * ffn_x, ln2_g[l], ln2_b[l])
    return x
```

```python
import functools
import math

import jax
import jax.numpy as jnp
from jax import lax
from jax.experimental import pallas as pl
from jax.experimental.pallas import tpu as pltpu

D_MODEL = 2048
DEPTH = 2
GRID_W = 64
CONV_W = 1024
CONV_K = 31
DA_HEADS = 8
DA_DH = 64
DA_DV = 2 * DA_DH
DA_W = DA_HEADS * DA_DV
ROPE_BASE = 10000.0
LRU_W = 1024
LRU_BLOCKS = 16
LRU_BS = LRU_W // LRU_BLOCKS
LRU_CONV_K = 4
LRU_C = 8.0
N_BRANCH = 3
N_EXPERTS = 32
TOP_K = 4
D_EXPERT = 1024
SWIGLU_LIMIT = 7.0
SWIGLU_ALPHA = 1.702
LN_EPS = 1e-5
DEEPNORM_ALPHA = (2 * DEPTH) ** 0.25
IN_SPLIT = (2 * CONV_W, DA_W, DA_W, DA_W, LRU_W, LRU_W, N_BRANCH * D_MODEL)
IN_W = sum(IN_SPLIT)

V7X_LANES = 128
V7X_SUBLANES = 8
V7X_VMEM_BYTES = 64 * 1024 * 1024

ZC_CONV_V, ZC_CONV_G, ZC_Q, ZC_K, ZC_V, ZC_LRU_X, ZC_LRU_G, ZC_MERGE = 0, 1, 2, 3, 4, 5, 6, 7
CONV_HALO = 16
MOE_ROWS = 256

BF16 = jnp.bfloat16
F32 = jnp.float32


def _params(sem, vmem_mb):
    return pltpu.CompilerParams(dimension_semantics=sem, vmem_limit_bytes=vmem_mb * 1024 * 1024)


def _ln(x):
    mu = jnp.mean(x, axis=-1, keepdims=True)
    xc = x - mu
    var = jnp.mean(xc * xc, axis=-1, keepdims=True)
    return xc * lax.rsqrt(var + LN_EPS)


class _Layout:
    def __init__(self, batch, seq, ctx_len):
        self.batch, self.seq, self.ctx = batch, seq, ctx_len
        self.s_all = ctx_len + seq
        self.n_all = batch * self.s_all
        self.tm = 256 if (ctx_len % 256 == 0 and seq % 256 == 0) else 128
        assert ctx_len % self.tm == 0 and seq % self.tm == 0
        self.tpb = self.s_all // self.tm
        self.ct = ctx_len // self.tm
        self.n_tiles = self.n_all // self.tm

    def mod_row(self, i):
        return jnp.where(i % self.tpb < self.ct, self.batch, i // self.tpb)


def _ada_kernel(c_ref, w_ref, b_ref, o_ref):
    c = c_ref[...]
    a = (c * jax.nn.sigmoid(c)).astype(BF16)
    o_ref[...] = jnp.dot(a, w_ref[...].astype(BF16), preferred_element_type=F32) + b_ref[...]


def _ada(c_rows, w, b):
    m, d = c_rows.shape
    n = w.shape[1]
    tn = 1024
    return pl.pallas_call(
        _ada_kernel,
        grid=(n // tn,),
        in_specs=[pl.BlockSpec((m, d), lambda j: (0, 0)),
                  pl.BlockSpec((d, tn), lambda j: (0, j)),
                  pl.BlockSpec((1, tn), lambda j: (0, j))],
        out_specs=pl.BlockSpec((m, tn), lambda j: (0, j)),
        out_shape=jax.ShapeDtypeStruct((m, n), F32),
        compiler_params=_params(("arbitrary",), 40),
        name="ada",
    )(c_rows, w, b.reshape(1, n))


def _ln_mod_kernel(x_ref, sh_ref, sc_ref, o_ref):
    o_ref[...] = (_ln(x_ref[...]) * (1.0 + sc_ref[...]) + sh_ref[...]).astype(o_ref.dtype)


def _ln_mod(lay, h, shift, scale):
    d = h.shape[1]
    mod_spec = pl.BlockSpec((None, 1, d), lambda i: (lay.mod_row(i), 0, 0))
    return pl.pallas_call(
        _ln_mod_kernel,
        grid=(lay.n_tiles,),
        in_specs=[pl.BlockSpec((lay.tm, d), lambda i: (i, 0)), mod_spec, mod_spec],
        out_specs=pl.BlockSpec((lay.tm, d), lambda i: (i, 0)),
        out_shape=jax.ShapeDtypeStruct(h.shape, BF16),
        compiler_params=_params(("arbitrary",), 32),
        name="ln_mod",
    )(h, shift, scale)


def _mm_kernel(x_ref, w_ref, b_ref, o_ref, wbf_ref):
    @pl.when(pl.program_id(1) == 0)
    def _():
        wbf_ref[...] = w_ref[...].astype(BF16)

    acc = jnp.dot(x_ref[...], wbf_ref[...], preferred_element_type=F32)
    o_ref[...] = (acc + b_ref[...]).astype(o_ref.dtype)


def _mm(x, w, b, tm, tn, out_dtype):
    m, k = x.shape
    n = w.shape[1]
    return pl.pallas_call(
        _mm_kernel,
        grid=(n // tn, m // tm),
        in_specs=[pl.BlockSpec((tm, k), lambda j, i: (i, 0)),
                  pl.BlockSpec((k, tn), lambda j, i: (0, j)),
                  pl.BlockSpec((1, tn), lambda j, i: (0, j))],
        out_specs=pl.BlockSpec((tm, tn), lambda j, i: (i, j)),
        out_shape=jax.ShapeDtypeStruct((m, n), out_dtype),
        scratch_shapes=[pltpu.VMEM((k, tn), BF16)],
        compiler_params=_params(("arbitrary", "arbitrary"), 48),
        name="in_proj",
    )(x, w, b.reshape(1, n))


def _rope_tables(lay):
    rows = lay.seq // GRID_W
    row = jnp.repeat(jnp.arange(rows, dtype=F32), GRID_W)
    col = jnp.tile(jnp.arange(GRID_W, dtype=F32), rows)
    n_freq = DA_DH // 4
    inv_freq = ROPE_BASE ** (-jnp.arange(n_freq, dtype=F32) / n_freq)
    ang = jnp.concatenate([row[:, None] * inv_freq, col[:, None] * inv_freq], axis=-1)
    cos, sin = jnp.cos(ang), jnp.sin(ang)
    cos_l = jnp.tile(cos, (1, 4))
    sin_l = jnp.tile(jnp.concatenate([-sin, sin], axis=-1), (1, 2))
    cos_l = jnp.concatenate([jnp.ones((lay.ctx, DA_DV), F32), cos_l], axis=0)
    sin_l = jnp.concatenate([jnp.zeros((lay.ctx, DA_DV), F32), sin_l], axis=0)
    return cos_l, sin_l


def _qkv_prep_kernel(zq_ref, zk_ref, zv_ref, cos_ref, sin_ref, q_ref, kt_ref, v_ref):
    cosf = cos_ref[...]
    sinf = sin_ref[...]
    lane = lax.broadcasted_iota(jnp.int32, cosf.shape, 1)
    first_half = (lane % DA_DH) < (DA_DH // 2)

    def rope(x):
        partner = jnp.where(first_half, pltpu.roll(x, DA_DV - DA_DH // 2, 1), pltpu.roll(x, DA_DH // 2, 1))
        return x * cosf + partner * sinf

    for h in range(DA_HEADS):
        sl = slice(h * DA_DV, (h + 1) * DA_DV)
        q_ref[:, sl] = (rope(zq_ref[:, sl]) * (DA_DH ** -0.5)).astype(BF16)
        kt_ref[sl, :] = rope(zk_ref[:, sl]).T.astype(BF16)
    v_ref[...] = zv_ref[...].astype(BF16)


def _qkv_prep(lay, z, cos_l, sin_l):
    tm = lay.tm
    zspec = lambda cb: pl.BlockSpec((tm, DA_W), lambda i: (i, cb))
    tspec = pl.BlockSpec((tm, DA_DV), lambda i: (i % lay.tpb, 0))
    return pl.pallas_call(
        _qkv_prep_kernel,
        grid=(lay.n_tiles,),
        in_specs=[zspec(ZC_Q), zspec(ZC_K), zspec(ZC_V), tspec, tspec],
        out_specs=[pl.BlockSpec((tm, DA_W), lambda i: (i, 0)),
                   pl.BlockSpec((None, DA_W, tm), lambda i: (i // lay.tpb, 0, i % lay.tpb)),
                   pl.BlockSpec((tm, DA_W), lambda i: (i, 0))],
        out_shape=[jax.ShapeDtypeStruct((lay.n_all, DA_W), BF16),
                   jax.ShapeDtypeStruct((lay.batch, DA_W, lay.s_all), BF16),
                   jax.ShapeDtypeStruct((lay.n_all, DA_W), BF16)],
        compiler_params=_params(("arbitrary",), 32),
        name="qkv_prep",
    )(z, z, z, cos_l, sin_l)


def _attn_kernel(lam_ref, q_ref, kt_ref, v_ref, g_ref, o_ref, *, ctx_tiles, ctx_len, lam_init):
    lp = lam_ref[...]
    lam = (jnp.exp(jnp.sum(lp[0:1] * lp[1:2], axis=-1, keepdims=True))
           - jnp.exp(jnp.sum(lp[2:3] * lp[3:4], axis=-1, keepdims=True)) + lam_init)

    def attend(n_k):
        q = q_ref[...]
        lane = lax.broadcasted_iota(jnp.int32, q.shape, 1)
        zero = jnp.zeros_like(q)
        kt = kt_ref[:, :n_k]
        v = v_ref[:n_k, :]

        def softmax_v(qm):
            s = jnp.dot(qm, kt, preferred_element_type=F32)
            p = jnp.exp(s - jnp.max(s, axis=-1, keepdims=True))
            l = jnp.sum(p, axis=-1, keepdims=True)
            return jnp.dot(p.astype(BF16), v, preferred_element_type=F32) / l

        o = softmax_v(jnp.where(lane < DA_DH, q, zero)) - lam * softmax_v(jnp.where(lane >= DA_DH, q, zero))
        y = o * lax.rsqrt(jnp.mean(o * o, axis=-1, keepdims=True) + LN_EPS)
        o_ref[...] = (y * g_ref[...] * (1.0 - lam_init)).astype(o_ref.dtype)

    is_ctx = pl.program_id(2) < ctx_tiles

    @pl.when(is_ctx)
    def _():
        attend(ctx_len)

    @pl.when(jnp.logical_not(is_ctx))
    def _():
        attend(kt_ref.shape[1])


def _attention(lay, q, kt, v, lam_p, norm_g, layer_idx):
    tq = lay.tm
    lam_init = 0.8 - 0.6 * math.exp(-0.3 * layer_idx)
    kern = functools.partial(_attn_kernel, ctx_tiles=lay.ct, ctx_len=lay.ctx, lam_init=lam_init)
    return pl.pallas_call(
        kern,
        grid=(lay.batch, DA_HEADS, lay.tpb),
        in_specs=[pl.BlockSpec((4, DA_DH), lambda b, h, i: (0, 0)),
                  pl.BlockSpec((tq, DA_DV), lambda b, h, i: (b * lay.tpb + i, h)),
                  pl.BlockSpec((None, DA_DV, lay.s_all), lambda b, h, i: (b, h, 0)),
                  pl.BlockSpec((lay.s_all, DA_DV), lambda b, h, i: (b, h)),
                  pl.BlockSpec((1, DA_DV), lambda b, h, i: (0, 0))],
        out_specs=pl.BlockSpec((tq, DA_DV), lambda b, h, i: (b * lay.tpb + i, h)),
        out_shape=jax.ShapeDtypeStruct((lay.n_all, DA_W), BF16),
        compiler_params=_params(("arbitrary", "arbitrary", "arbitrary"), 56),
        name="diff_attn",
    )(lam_p, q, kt, v, norm_g.reshape(1, DA_DV))


def _conv_kernel(v_ref, g_ref, vp_ref, gp_ref, vn_ref, gn_ref, dw_ref, db_ref, lg_ref, lb_ref,
                 o_ref, hp_ref, acc_ref, *, tpb, ct):
    tm = v_ref.shape[0]
    ti = pl.program_id(0) % tpb
    has_prev = jnp.logical_and(ti != 0, ti != ct)
    has_next = jnp.logical_and(ti != ct - 1, ti != tpb - 1)
    glu = lambda v, g: v * jax.nn.sigmoid(g)
    hp_ref[0:CONV_HALO, :] = jnp.where(has_prev, glu(vp_ref[...], gp_ref[...]), 0.0)
    hp_ref[CONV_HALO:CONV_HALO + tm, :] = glu(v_ref[...], g_ref[...])
    hp_ref[CONV_HALO + tm:, :] = jnp.where(has_next, glu(vn_ref[...], gn_ref[...]), 0.0)

    first = CONV_HALO - CONV_K // 2
    for c in range(CONV_W // V7X_LANES):
        cs = pl.ds(c * V7X_LANES, V7X_LANES)
        acc = jnp.zeros((tm, V7X_LANES), F32)
        for k in range(CONV_K):
            acc = acc + dw_ref[k:k + 1, cs] * hp_ref[pl.ds(first + k, tm), cs]
        acc_ref[:, cs] = acc
    y = _ln(acc_ref[...] + db_ref[...]) * lg_ref[...] + lb_ref[...]
    o_ref[...] = (y * jax.nn.sigmoid(y)).astype(o_ref.dtype)


def _conv_branch(lay, z, dw, db, ln_g, ln_b):
    tm = lay.tm
    hb = tm // CONV_HALO
    n_halo = lay.n_all // CONV_HALO
    main = lambda cb: pl.BlockSpec((tm, CONV_W), lambda i: (i, cb))
    prev = lambda cb: pl.BlockSpec((CONV_HALO, CONV_W), lambda i: (jnp.maximum(i * hb - 1, 0), cb))
    nxt = lambda cb: pl.BlockSpec((CONV_HALO, CONV_W), lambda i: (jnp.minimum((i + 1) * hb, n_halo - 1), cb))
    vec = pl.BlockSpec((1, CONV_W), lambda i: (0, 0))
    kern = functools.partial(_conv_kernel, tpb=lay.tpb, ct=lay.ct)
    return pl.pallas_call(
        kern,
        grid=(lay.n_tiles,),
        in_specs=[main(ZC_CONV_V), main(ZC_CONV_G), prev(ZC_CONV_V), prev(ZC_CONV_G),
                  nxt(ZC_CONV_V), nxt(ZC_CONV_G),
                  pl.BlockSpec((CONV_K, CONV_W), lambda i: (0, 0)), vec, vec, vec],
        out_specs=pl.BlockSpec((tm, CONV_W), lambda i: (i, 0)),
        out_shape=jax.ShapeDtypeStruct((lay.n_all, CONV_W), BF16),
        scratch_shapes=[pltpu.VMEM((tm + 2 * CONV_HALO, CONV_W), F32), pltpu.VMEM((tm, CONV_W), F32)],
        compiler_params=_params(("arbitrary",), 32),
        name="conv_branch",
    )(z, z, z, z, z, z, dw, db.reshape(1, -1), ln_g.reshape(1, -1), ln_b.reshape(1, -1))


LRU_CH = V7X_LANES
LRU_PAD = V7X_SUBLANES


def _gelu_tanh(x):
    return 0.5 * x * (1.0 + jnp.tanh(math.sqrt(2.0 / math.pi) * (x + 0.044715 * (x * x * x))))


def _softplus(x):
    return jnp.maximum(x, 0.0) + jnp.log1p(jnp.exp(-jnp.abs(x)))


def _tile_scan(a, b, reverse):
    row = lax.broadcasted_iota(jnp.int32, a.shape, 1)
    for s in (1, 2, 4):
        if reverse:
            valid = row < V7X_SUBLANES - s
            shift = V7X_SUBLANES - s
        else:
            valid = row >= s
            shift = s
        a_sh = pltpu.roll(a, shift, 1)
        b_sh = pltpu.roll(b, shift, 1)
        b = jnp.where(valid, a * b_sh + b, b)
        a = jnp.where(valid, a * a_sh, a)
    return a, b


def _lru_kernel(zx_ref, zg_ref, cw_ref, cb_ref, gw_ref, gb_ref, lam_ref, o_ref,
                xp_ref, af_ref, bf_ref, ab_ref, bb_ref, *, ctx_len, chunk):
    s_all = zx_ref.shape[0]
    ch = zx_ref.shape[1]
    xp_ref[0:LRU_PAD, :] = jnp.zeros((LRU_PAD, ch), F32)
    xp_ref[LRU_PAD:LRU_PAD + s_all, :] = zx_ref[...]
    xp_ref[LRU_PAD + s_all:, :] = jnp.zeros((LRU_PAD, ch), F32)

    sp = _softplus(-lam_ref[...])
    gw = gw_ref[...]
    seg_starts = (0, ctx_len)
    seg_ends = (ctx_len, s_all)
    for r0 in range(0, s_all, chunk):
        row = r0 + lax.broadcasted_iota(jnp.int32, (chunk, ch), 0)
        xs = jnp.zeros((chunk, ch), F32) + cb_ref[...]
        for k in range(LRU_CONV_K):
            off = k - 2
            x = xp_ref[pl.ds(LRU_PAD + r0 + off, chunk), :]
            if off < 0 and r0 in seg_starts:
                x = jnp.where(row - r0 < -off, 0.0, x)
            if off > 0 and r0 + chunk in seg_ends:
                x = jnp.where(row - r0 >= chunk - off, 0.0, x)
            xs = xs + cw_ref[k:k + 1, :] * x
        pre = jnp.dot(xs.astype(BF16), gw, preferred_element_type=F32) + gb_ref[...]
        for d, (a_ref, b_ref) in enumerate(((af_ref, bf_ref), (ab_ref, bb_ref))):
            r = jax.nn.sigmoid(pre[:, (2 * d) * ch:(2 * d + 1) * ch])
            i = jax.nn.sigmoid(pre[:, (2 * d + 1) * ch:(2 * d + 2) * ch])
            a = jnp.exp(-LRU_C * r * sp[d:d + 1, :])
            b = jnp.sqrt(1.0 - a * a) * (i * xs)
            a_t, b_t = _tile_scan(a.reshape(chunk // V7X_SUBLANES, V7X_SUBLANES, ch),
                                  b.reshape(chunk // V7X_SUBLANES, V7X_SUBLANES, ch), reverse=(d == 1))
            a_ref[pl.ds(r0, chunk), :] = a_t.reshape(chunk, ch)
            b_ref[pl.ds(r0, chunk), :] = b_t.reshape(chunk, ch)

    n_t = s_all // V7X_SUBLANES
    c_t = ctx_len // V7X_SUBLANES

    def fwd_step(t, carry):
        rows = pl.ds(pl.multiple_of(t * V7X_SUBLANES, V7X_SUBLANES), V7X_SUBLANES)
        h = bf_ref[rows, :] + af_ref[rows, :] * carry
        bf_ref[rows, :] = h
        return jnp.broadcast_to(h[V7X_SUBLANES - 1:, :], h.shape)

    def bwd_step(t, carry, hi):
        t = hi - 1 - t
        rows = pl.ds(pl.multiple_of(t * V7X_SUBLANES, V7X_SUBLANES), V7X_SUBLANES)
        h = bb_ref[rows, :] + ab_ref[rows, :] * carry
        bb_ref[rows, :] = h
        return jnp.broadcast_to(h[0:1, :], h.shape)

    zero = jnp.zeros((V7X_SUBLANES, ch), F32)
    lax.fori_loop(0, n_t, fwd_step, zero)
    carry = lax.fori_loop(0, c_t, functools.partial(bwd_step, hi=c_t), zero)
    lax.fori_loop(0, n_t - c_t, functools.partial(bwd_step, hi=n_t), carry)

    for r0 in range(0, s_all, chunk):
        rows = pl.ds(r0, chunk)
        hsum = bf_ref[rows, :] + bb_ref[rows, :]
        o_ref[rows, :] = (hsum * _gelu_tanh(zg_ref[rows, :])).astype(o_ref.dtype)


def _lru_gate_matrix(gate_w):
    n_grp = LRU_W // LRU_CH
    per = LRU_CH // LRU_BS
    w = gate_w.reshape(4, n_grp, per, LRU_BS, LRU_BS)
    eye = jnp.eye(per, dtype=gate_w.dtype)
    full = jnp.einsum("gcpjk,pq->cpjgqk", w, eye)
    return full.reshape(n_grp, LRU_CH, 4 * LRU_CH)


def _lru_branch(lay, z, conv_w, conv_b, gate_w, gate_b, lam):
    s_all = lay.s_all
    n_grp = LRU_W // LRU_CH
    chunk = lay.tm
    gw = _lru_gate_matrix(gate_w).astype(BF16)
    gb = gate_b.reshape(4, n_grp, LRU_CH).transpose(1, 0, 2).reshape(n_grp, 1, 4 * LRU_CH)
    kern = functools.partial(_lru_kernel, ctx_len=lay.ctx, chunk=chunk)
    cb_x = ZC_LRU_X * (1024 // LRU_CH)
    cb_g = ZC_LRU_G * (1024 // LRU_CH)
    return pl.pallas_call(
        kern,
        grid=(lay.batch, n_grp),
        in_specs=[pl.BlockSpec((s_all, LRU_CH), lambda b, c: (b, cb_x + c)),
                  pl.BlockSpec((s_all, LRU_CH), lambda b, c: (b, cb_g + c)),
                  pl.BlockSpec((LRU_CONV_K, LRU_CH), lambda b, c: (0, c)),
                  pl.BlockSpec((1, LRU_CH), lambda b, c: (0, c)),
                  pl.BlockSpec((None, LRU_CH, 4 * LRU_CH), lambda b, c: (c, 0, 0)),
                  pl.BlockSpec((None, 1, 4 * LRU_CH), lambda b, c: (c, 0, 0)),
                  pl.BlockSpec((2, LRU_CH), lambda b, c: (0, c))],
        out_specs=pl.BlockSpec((s_all, LRU_CH), lambda b, c: (b, c)),
        out_shape=jax.ShapeDtypeStruct((lay.n_all, LRU_W), BF16),
        scratch_shapes=[pltpu.VMEM((s_all + 2 * LRU_PAD, LRU_CH), F32)]
        + [pltpu.VMEM((s_all, LRU_CH), F32) for _ in range(4)],
        compiler_params=_params(("arbitrary", "arbitrary"), 48),
        name="rglru",
    )(z, z, conv_w, conv_b.reshape(1, -1), gw, gb, lam)


def _merge_kernel(hc_ref, ha_ref, hl_ref, g0a, g0b, g1a, g1b, g2a, g2b, wc_ref, wa_ref, wl_ref, o_ref):
    half = D_MODEL // 2
    for h_ref, w_ref, ga, gb, first in ((hc_ref, wc_ref, g0a, g0b, True), (ha_ref, wa_ref, g1a, g1b, False),
                                        (hl_ref, wl_ref, g2a, g2b, False)):
        h = h_ref[...]
        for half_idx, g_ref in enumerate((ga, gb)):
            cs = slice(half_idx * half, (half_idx + 1) * half)
            p = jax.nn.sigmoid(g_ref[...]) * jnp.dot(h, w_ref[:, cs], preferred_element_type=F32)
            if first:
                o_ref[:, cs] = p
            else:
                o_ref[:, cs] = o_ref[:, cs] + p


def _merge(lay, z, hc, ha, hl, wc, wa, wl):
    tm = lay.tm
    hspec = pl.BlockSpec((tm, 1024), lambda i: (i, 0))
    gspec = lambda cb: pl.BlockSpec((tm, 1024), lambda i: (i, ZC_MERGE + cb))
    wspec = pl.BlockSpec((1024, D_MODEL), lambda i: (0, 0))
    return pl.pallas_call(
        _merge_kernel,
        grid=(lay.n_tiles,),
        in_specs=[hspec, hspec, hspec] + [gspec(cb) for cb in range(6)] + [wspec, wspec, wspec],
        out_specs=pl.BlockSpec((tm, D_MODEL), lambda i: (i, 0)),
        out_shape=jax.ShapeDtypeStruct((lay.n_all, D_MODEL), F32),
        compiler_params=_params(("arbitrary",), 56),
        name="merge",
    )(hc, ha, hl, z, z, z, z, z, z, wc, wa, wl)


def _mix_kernel(m_ref, h_ref, wo_ref, g1_ref, lg_ref, lb_ref, sh_ref, sc_ref, rw_ref, rb_ref,
                h_out, u_out, idx_out, wgt_out):
    mix = jnp.dot(m_ref[...].astype(BF16), wo_ref[...], preferred_element_type=F32)
    x = _ln(DEEPNORM_ALPHA * h_ref[...] + g1_ref[...] * mix) * lg_ref[...] + lb_ref[...]
    h_out[...] = x
    u = _ln(x) * (1.0 + sc_ref[...]) + sh_ref[...]
    u_out[...] = u.astype(u_out.dtype)

    logits = jnp.dot(u, rw_ref[...], preferred_element_type=F32, precision=lax.Precision.HIGHEST) + rb_ref[...]
    lane = lax.broadcasted_iota(jnp.int32, logits.shape, 1).astype(F32)
    out_lane = lax.broadcasted_iota(jnp.int32, idx_out.shape, 1)
    idx_acc = jnp.zeros(idx_out.shape, jnp.int32)
    val_acc = jnp.zeros(wgt_out.shape, F32)
    top = None
    denom = None
    for k in range(TOP_K):
        mx = jnp.max(logits, axis=-1, keepdims=True)
        idx = jnp.min(jnp.where(logits == mx, lane, float(N_EXPERTS)), axis=-1, keepdims=True)
        logits = jnp.where(lane == idx, -jnp.inf, logits)
        idx = idx.astype(jnp.int32)
        if k == 0:
            top = mx
        e = jnp.exp(mx - top)
        denom = e if k == 0 else denom + e
        idx_acc = jnp.where(out_lane == k, idx, idx_acc)
        val_acc = jnp.where(out_lane == k, e, val_acc)
    idx_out[...] = idx_acc
    wgt_out[...] = val_acc / denom


def _mix(lay, m, h, wo, g1, ln_g, ln_b, sh2, sc2, router_w, router_b):
    tm = lay.tm
    d = D_MODEL
    row = pl.BlockSpec((tm, d), lambda i: (i, 0))
    mod = pl.BlockSpec((None, 1, d), lambda i: (lay.mod_row(i), 0, 0))
    vec = pl.BlockSpec((1, d), lambda i: (0, 0))
    small = pl.BlockSpec((tm, V7X_LANES), lambda i: (i, 0))
    return pl.pallas_call(
        _mix_kernel,
        grid=(lay.n_tiles,),
        in_specs=[row, row, pl.BlockSpec((d, d), lambda i: (0, 0)), mod, vec, vec, mod, mod,
                  pl.BlockSpec((d, N_EXPERTS), lambda i: (0, 0)),
                  pl.BlockSpec((1, N_EXPERTS), lambda i: (0, 0))],
        out_specs=[row, row, small, small],
        out_shape=[jax.ShapeDtypeStruct((lay.n_all, d), F32),
                   jax.ShapeDtypeStruct((lay.n_all, d), BF16),
                   jax.ShapeDtypeStruct((lay.n_all, V7X_LANES), jnp.int32),
                   jax.ShapeDtypeStruct((lay.n_all, V7X_LANES), F32)],
        compiler_params=_params(("arbitrary",), 56),
        name="mix_ln1_router",
    )(m, h, wo, g1, ln_g.reshape(1, d), ln_b.reshape(1, d), sh2, sc2, router_w, router_b.reshape(1, -1))


def _moe_kernel(blk_e_ref, n_used_ref, x_ref, w1g_ref, w1l_ref, b1g_ref, b1l_ref, w2_ref, b2_ref, o_ref):
    del blk_e_ref
    used = pl.program_id(0) < n_used_ref[0]

    @pl.when(used)
    def _():
        x = x_ref[...]
        gl = jnp.minimum(jnp.dot(x, w1g_ref[...], preferred_element_type=F32) + b1g_ref[...], SWIGLU_LIMIT)
        lin = jnp.clip(jnp.dot(x, w1l_ref[...], preferred_element_type=F32) + b1l_ref[...],
                       -SWIGLU_LIMIT, SWIGLU_LIMIT)
        act = gl * jax.nn.sigmoid(SWIGLU_ALPHA * gl) * (lin + 1.0)
        o_ref[...] = jnp.dot(act.astype(BF16), w2_ref[...], preferred_element_type=F32) + b2_ref[...]

    @pl.when(jnp.logical_not(used))
    def _():
        o_ref[...] = jnp.zeros_like(o_ref)


def _moe_ffn_blocks(xs, blk_e, n_used, w1g, w1l, b1g, b1l, w2, b2):
    n_slots, d = xs.shape
    n_blocks = n_slots // MOE_ROWS
    f = w1g.shape[2]
    wsel = lambda shape: pl.BlockSpec((None,) + shape, lambda i, be, nu: (be[i], 0, 0))
    grid_spec = pltpu.PrefetchScalarGridSpec(
        num_scalar_prefetch=2,
        grid=(n_blocks,),
        in_specs=[pl.BlockSpec((MOE_ROWS, d), lambda i, be, nu: (i, 0)),
                  wsel((d, f)), wsel((d, f)), wsel((1, f)), wsel((1, f)), wsel((f, d)), wsel((1, d))],
        out_specs=pl.BlockSpec((MOE_ROWS, d), lambda i, be, nu: (i, 0)),
    )
    return pl.pallas_call(
        _moe_kernel,
        grid_spec=grid_spec,
        out_shape=jax.ShapeDtypeStruct((n_slots, d), F32),
        compiler_params=_params(("arbitrary",), 56),
        name="moe_ffn",
    )(blk_e, n_used, xs, w1g, w1l, b1g, b1l, w2, b2)


def _moe_dispatch(top_idx, n_tok):
    n_assign = n_tok * TOP_K
    n_blocks = -(-n_assign // MOE_ROWS) + N_EXPERTS
    n_slots = n_blocks * MOE_ROWS
    e_flat = top_idx.reshape(-1)
    tok_flat = jnp.arange(n_assign, dtype=jnp.int32) // TOP_K
    order = jnp.argsort(e_flat)
    e_sorted = e_flat[order]
    counts = jnp.zeros((N_EXPERTS,), jnp.int32).at[e_flat].add(1)
    padded = (counts + MOE_ROWS - 1) // MOE_ROWS * MOE_ROWS
    start = jnp.cumsum(counts) - counts
    pad_end = jnp.cumsum(padded)
    pad_start = pad_end - padded
    dest = pad_start[e_sorted] + jnp.arange(n_assign, dtype=jnp.int32) - start[e_sorted]
    slot_tok = jnp.zeros((n_slots,), jnp.int32).at[dest].set(tok_flat[order])
    slot_of = jnp.zeros((n_assign,), jnp.int32).at[order].set(dest)
    blk_start = jnp.arange(n_blocks, dtype=jnp.int32) * MOE_ROWS
    blk_e = jnp.minimum(jnp.searchsorted(pad_end, blk_start, side="right"), N_EXPERTS - 1).astype(jnp.int32)
    n_used = (pad_end[-1] // MOE_ROWS).astype(jnp.int32).reshape(1)
    return slot_tok, slot_of.reshape(n_tok, TOP_K), blk_e, n_used


def _final_kernel(h_ref, y_ref, w_ref, g2_ref, lg_ref, lb_ref, *rest, with_next):
    if with_next:
        sh_ref, sc_ref, h_out, u_out = rest
    else:
        (h_out,) = rest
    w = w_ref[...]
    y = w[:, 0:1] * y_ref[0]
    for k in range(1, TOP_K):
        y = y + w[:, k:k + 1] * y_ref[k]
    x = _ln(DEEPNORM_ALPHA * h_ref[...] + g2_ref[...] * y) * lg_ref[...] + lb_ref[...]
    h_out[...] = x
    if with_next:
        u_out[...] = (_ln(x) * (1.0 + sc_ref[...]) + sh_ref[...]).astype(u_out.dtype)


def _final(lay, h, yk, top_w, g2, ln_g, ln_b, next_mod):
    tm = lay.tm
    d = D_MODEL
    row = pl.BlockSpec((tm, d), lambda i: (i, 0))
    mod = pl.BlockSpec((None, 1, d), lambda i: (lay.mod_row(i), 0, 0))
    vec = pl.BlockSpec((1, d), lambda i: (0, 0))
    in_specs = [row, pl.BlockSpec((TOP_K, tm, d), lambda i: (0, i, 0)),
                pl.BlockSpec((tm, V7X_LANES), lambda i: (i, 0)), mod, vec, vec]
    args = [h, yk, top_w, g2, ln_g.reshape(1, d), ln_b.reshape(1, d)]
    out_specs = [row]
    out_shape = [jax.ShapeDtypeStruct((lay.n_all, d), F32)]
    if next_mod is not None:
        in_specs += [mod, mod]
        args += list(next_mod)
        out_specs.append(row)
        out_shape.append(jax.ShapeDtypeStruct((lay.n_all, d), BF16))
    outs = pl.pallas_call(
        functools.partial(_final_kernel, with_next=next_mod is not None),
        grid=(lay.n_tiles,),
        in_specs=in_specs,
        out_specs=out_specs,
        out_shape=out_shape,
        compiler_params=_params(("arbitrary",), 48),
        name="combine_ln2",
    )(*args)
    return outs if next_mod is not None else (outs[0], None)


def kernel(x, c, ctx, c_ctx, w_ada, b_ada, w_in, b_in, conv_dw, conv_db, conv_ln_g, conv_ln_b, w_conv_out,
           da_lambda, da_norm_g, w_da_out, lru_conv_w, lru_conv_b, lru_gate_w, lru_gate_b, lru_lambda,
           w_lru_out, w_out, ln1_g, ln1_b, router_w, router_b, moe_w1, moe_b1, moe_w2, moe_b2, ln2_g, ln2_b):
    batch, seq, d = x.shape
    lay = _Layout(batch, seq, ctx.shape[1])
    h = jnp.concatenate([ctx, x], axis=1).reshape(lay.n_all, d)

    c_rows = jnp.zeros((16, d), F32).at[:batch].set(c).at[batch].set(c_ctx)
    mods = []
    for l in range(DEPTH):
        ada = _ada(c_rows, w_ada[l], b_ada[l])[:batch + 1]
        mods.append([m.reshape(batch + 1, 1, d) for m in jnp.split(ada, 6, axis=-1)])

    cos_l, sin_l = _rope_tables(lay)
    u = _ln_mod(lay, h, mods[0][0], mods[0][1])
    for l in range(DEPTH):
        sh1, sc1, g1, sh2, sc2, g2 = mods[l]
        tm_in = 512 if lay.n_all % 512 == 0 else lay.tm
        z = _mm(u, w_in[l], b_in[l], tm_in, 1024, F32)

        hc = _conv_branch(lay, z, conv_dw[l], conv_db[l], conv_ln_g[l], conv_ln_b[l])
        q, kt, v = _qkv_prep(lay, z, cos_l, sin_l)
        ha = _attention(lay, q, kt, v, da_lambda[l], da_norm_g[l], l)
        hl = _lru_branch(lay, z, lru_conv_w[l], lru_conv_b[l], lru_gate_w[l], lru_gate_b[l], lru_lambda[l])
        m = _merge(lay, z, hc, ha, hl, w_conv_out[l].astype(BF16), w_da_out[l].astype(BF16),
                   w_lru_out[l].astype(BF16))
        h, u2, top_idx, top_w = _mix(lay, m, h, w_out[l].astype(BF16), g1, ln1_g[l], ln1_b[l], sh2, sc2,
                                     router_w[l], router_b[l])

        slot_tok, slot_of, blk_e, n_used = _moe_dispatch(top_idx[:, :TOP_K], lay.n_all)
        xs = jnp.take(u2, slot_tok, axis=0)
        w1 = moe_w1[l]
        outs = _moe_ffn_blocks(xs, blk_e, n_used,
                               w1[:, :, 0::2].astype(BF16), w1[:, :, 1::2].astype(BF16),
                               moe_b1[l][:, None, 0::2], moe_b1[l][:, None, 1::2],
                               moe_w2[l].astype(BF16), moe_b2[l][:, None, :])
        yk = jnp.take(outs, slot_of.T, axis=0)
        next_mod = (mods[l + 1][0], mods[l + 1][1]) if l + 1 < DEPTH else None
        h, u = _final(lay, h, yk, top_w, g2, ln2_g[l], ln2_b[l], next_mod)

    return h.reshape(batch, lay.s_all, d)[:, lay.ctx:, :]
```

```python
import functools
import math

import jax
import jax.numpy as jnp
from jax import lax
from jax.experimental import pallas as pl
from jax.experimental.pallas import tpu as pltpu

D_MODEL = 2048
DEPTH = 2
GRID_W = 64
CONV_W = 1024
CONV_K = 31
DA_HEADS = 8
DA_DH = 64
DA_DV = 2 * DA_DH
DA_W = DA_HEADS * DA_DV
ROPE_BASE = 10000.0
LRU_W = 1024
LRU_BLOCKS = 16
LRU_BS = LRU_W // LRU_BLOCKS
LRU_CONV_K = 4
LRU_C = 8.0
N_BRANCH = 3
N_EXPERTS = 32
TOP_K = 4
D_EXPERT = 1024
SWIGLU_LIMIT = 7.0
SWIGLU_ALPHA = 1.702
LN_EPS = 1e-5
DEEPNORM_ALPHA = (2 * DEPTH) ** 0.25
IN_SPLIT = (2 * CONV_W, DA_W, DA_W, DA_W, LRU_W, LRU_W, N_BRANCH * D_MODEL)
IN_W = sum(IN_SPLIT)

V7X_LANES = 128
V7X_SUBLANES = 8
V7X_VMEM_BYTES = 64 * 1024 * 1024

ZC_CONV_V, ZC_CONV_G, ZC_Q, ZC_K, ZC_V, ZC_LRU_X, ZC_LRU_G, ZC_MERGE = 0, 1, 2, 3, 4, 5, 6, 7
CONV_HALO = 16
MOE_ROWS = 256

BF16 = jnp.bfloat16
F32 = jnp.float32


def _params(sem, vmem_mb):
    return pltpu.CompilerParams(dimension_semantics=sem, vmem_limit_bytes=vmem_mb * 1024 * 1024)


def _ln(x):
    mu = jnp.mean(x, axis=-1, keepdims=True)
    xc = x - mu
    var = jnp.mean(xc * xc, axis=-1, keepdims=True)
    return xc * lax.rsqrt(var + LN_EPS)


class _Layout:
    def __init__(self, batch, seq, ctx_len):
        self.batch, self.seq, self.ctx = batch, seq, ctx_len
        self.s_all = ctx_len + seq
        self.n_all = batch * self.s_all
        self.tm = 256 if (ctx_len % 256 == 0 and seq % 256 == 0) else 128
        assert ctx_len % self.tm == 0 and seq % self.tm == 0
        self.tpb = self.s_all // self.tm
        self.ct = ctx_len // self.tm
        self.n_tiles = self.n_all // self.tm

    def mod_row(self, i):
        return jnp.where(i % self.tpb < self.ct, self.batch, i // self.tpb)


def _ada_kernel(c_ref, w_ref, b_ref, o_ref):
    c = c_ref[...]
    a = (c * jax.nn.sigmoid(c)).astype(BF16)
    o_ref[...] = jnp.dot(a, w_ref[...].astype(BF16), preferred_element_type=F32) + b_ref[...]


def _ada(c_rows, w, b):
    m, d = c_rows.shape
    n = w.shape[1]
    tn = 1024
    return pl.pallas_call(
        _ada_kernel,
        grid=(n // tn,),
        in_specs=[pl.BlockSpec((m, d), lambda j: (0, 0)),
                  pl.BlockSpec((d, tn), lambda j: (0, j)),
                  pl.BlockSpec((1, tn), lambda j: (0, j))],
        out_specs=pl.BlockSpec((m, tn), lambda j: (0, j)),
        out_shape=jax.ShapeDtypeStruct((m, n), F32),
        compiler_params=_params(("arbitrary",), 40),
        name="ada",
    )(c_rows, w, b.reshape(1, n))


def _ln_mod_kernel(x_ref, sh_ref, sc_ref, o_ref):
    o_ref[...] = (_ln(x_ref[...]) * (1.0 + sc_ref[...]) + sh_ref[...]).astype(o_ref.dtype)


def _ln_mod(lay, h, shift, scale):
    d = h.shape[1]
    mod_spec = pl.BlockSpec((None, 1, d), lambda i: (lay.mod_row(i), 0, 0))
    return pl.pallas_call(
        _ln_mod_kernel,
        grid=(lay.n_tiles,),
        in_specs=[pl.BlockSpec((lay.tm, d), lambda i: (i, 0)), mod_spec, mod_spec],
        out_specs=pl.BlockSpec((lay.tm, d), lambda i: (i, 0)),
        out_shape=jax.ShapeDtypeStruct(h.shape, BF16),
        compiler_params=_params(("arbitrary",), 32),
        name="ln_mod",
    )(h, shift, scale)


def _mm_kernel(x_ref, w_ref, b_ref, o_ref, wbf_ref):
    @pl.when(pl.program_id(1) == 0)
    def _():
        wbf_ref[...] = w_ref[...].astype(BF16)

    acc = jnp.dot(x_ref[...], wbf_ref[...], preferred_element_type=F32)
    o_ref[...] = (acc + b_ref[...]).astype(o_ref.dtype)


def _mm(x, w, b, tm, tn, out_dtype):
    m, k = x.shape
    n = w.shape[1]
    return pl.pallas_call(
        _mm_kernel,
        grid=(n // tn, m // tm),
        in_specs=[pl.BlockSpec((tm, k), lambda j, i: (i, 0)),
                  pl.BlockSpec((k, tn), lambda j, i: (0, j)),
                  pl.BlockSpec((1, tn), lambda j, i: (0, j))],
        out_specs=pl.BlockSpec((tm, tn), lambda j, i: (i, j)),
        out_shape=jax.ShapeDtypeStruct((m, n), out_dtype),
        scratch_shapes=[pltpu.VMEM((k, tn), BF16)],
        compiler_params=_params(("arbitrary", "arbitrary"), 48),
        name="in_proj",
    )(x, w, b.reshape(1, n))


def _rope_tables(lay):
    rows = lay.seq // GRID_W
    row = jnp.repeat(jnp.arange(rows, dtype=F32), GRID_W)
    col = jnp.tile(jnp.arange(GRID_W, dtype=F32), rows)
    n_freq = DA_DH // 4
    inv_freq = ROPE_BASE ** (-jnp.arange(n_freq, dtype=F32) / n_freq)
    ang = jnp.concatenate([row[:, None] * inv_freq, col[:, None] * inv_freq], axis=-1)
    cos, sin = jnp.cos(ang), jnp.sin(ang)
    cos_l = jnp.tile(cos, (1, 4))
    sin_l = jnp.tile(jnp.concatenate([-sin, sin], axis=-1), (1, 2))
    cos_l = jnp.concatenate([jnp.ones((lay.ctx, DA_DV), F32), cos_l], axis=0)
    sin_l = jnp.concatenate([jnp.zeros((lay.ctx, DA_DV), F32), sin_l], axis=0)
    return cos_l, sin_l


def _qkv_prep_kernel(zq_ref, zk_ref, zv_ref, cos_ref, sin_ref, q_ref, kt_ref, v_ref):
    cosf = cos_ref[...]
    sinf = sin_ref[...]
    lane = lax.broadcasted_iota(jnp.int32, cosf.shape, 1)
    first_half = (lane % DA_DH) < (DA_DH // 2)

    def rope(x):
        partner = jnp.where(first_half, pltpu.roll(x, DA_DV - DA_DH // 2, 1), pltpu.roll(x, DA_DH // 2, 1))
        return x * cosf + partner * sinf

    for h in range(DA_HEADS):
        sl = slice(h * DA_DV, (h + 1) * DA_DV)
        q_ref[:, sl] = (rope(zq_ref[:, sl]) * (DA_DH ** -0.5)).astype(BF16)
        kt_ref[sl, :] = rope(zk_ref[:, sl]).T.astype(BF16)
    v_ref[...] = zv_ref[...].astype(BF16)


def _qkv_prep(lay, z, cos_l, sin_l):
    tm = lay.tm
    zspec = lambda cb: pl.BlockSpec((tm, DA_W), lambda i: (i, cb))
    tspec = pl.BlockSpec((tm, DA_DV), lambda i: (i % lay.tpb, 0))
    return pl.pallas_call(
        _qkv_prep_kernel,
        grid=(lay.n_tiles,),
        in_specs=[zspec(ZC_Q), zspec(ZC_K), zspec(ZC_V), tspec, tspec],
        out_specs=[pl.BlockSpec((tm, DA_W), lambda i: (i, 0)),
                   pl.BlockSpec((None, DA_W, tm), lambda i: (i // lay.tpb, 0, i % lay.tpb)),
                   pl.BlockSpec((tm, DA_W), lambda i: (i, 0))],
        out_shape=[jax.ShapeDtypeStruct((lay.n_all, DA_W), BF16),
                   jax.ShapeDtypeStruct((lay.batch, DA_W, lay.s_all), BF16),
                   jax.ShapeDtypeStruct((lay.n_all, DA_W), BF16)],
        compiler_params=_params(("arbitrary",), 32),
        name="qkv_prep",
    )(z, z, z, cos_l, sin_l)


def _attn_kernel(lam_ref, q_ref, kt_ref, v_ref, g_ref, o_ref, *, ctx_tiles, ctx_len, lam_init):
    lp = lam_ref[...]
    lam = (jnp.exp(jnp.sum(lp[0:1] * lp[1:2], axis=-1, keepdims=True))
           - jnp.exp(jnp.sum(lp[2:3] * lp[3:4], axis=-1, keepdims=True)) + lam_init)

    def attend(n_k):
        q = q_ref[...]
        lane = lax.broadcasted_iota(jnp.int32, q.shape, 1)
        zero = jnp.zeros_like(q)
        kt = kt_ref[:, :n_k]
        v = v_ref[:n_k, :]

        def softmax_v(qm):
            s = jnp.dot(qm, kt, preferred_element_type=F32)
            p = jnp.exp(s - jnp.max(s, axis=-1, keepdims=True))
            l = jnp.sum(p, axis=-1, keepdims=True)
            return jnp.dot(p.astype(BF16), v, preferred_element_type=F32) / l

        o = softmax_v(jnp.where(lane < DA_DH, q, zero)) - lam * softmax_v(jnp.where(lane >= DA_DH, q, zero))
        y = o * lax.rsqrt(jnp.mean(o * o, axis=-1, keepdims=True) + LN_EPS)
        o_ref[...] = (y * g_ref[...] * (1.0 - lam_init)).astype(o_ref.dtype)

    is_ctx = pl.program_id(2) < ctx_tiles

    @pl.when(is_ctx)
    def _():
        attend(ctx_len)

    @pl.when(jnp.logical_not(is_ctx))
    def _():
        attend(kt_ref.shape[1])


def _attention(lay, q, kt, v, lam_p, norm_g, layer_idx):
    tq = lay.tm
    lam_init = 0.8 - 0.6 * math.exp(-0.3 * layer_idx)
    kern = functools.partial(_attn_kernel, ctx_tiles=lay.ct, ctx_len=lay.ctx, lam_init=lam_init)
    return pl.pallas_call(
        kern,
        grid=(lay.batch, DA_HEADS, lay.tpb),
        in_specs=[pl.BlockSpec((4, DA_DH), lambda b, h, i: (0, 0)),
                  pl.BlockSpec((tq, DA_DV), lambda b, h, i: (b * lay.tpb + i, h)),
                  pl.BlockSpec((None, DA_DV, lay.s_all), lambda b, h, i: (b, h, 0)),
                  pl.BlockSpec((lay.s_all, DA_DV), lambda b, h, i: (b, h)),
                  pl.BlockSpec((1, DA_DV), lambda b, h, i: (0, 0))],
        out_specs=pl.BlockSpec((tq, DA_DV), lambda b, h, i: (b * lay.tpb + i, h)),
        out_shape=jax.ShapeDtypeStruct((lay.n_all, DA_W), BF16),
        compiler_params=_params(("arbitrary", "arbitrary", "arbitrary"), 56),
        name="diff_attn",
    )(lam_p, q, kt, v, norm_g.reshape(1, DA_DV))


def _conv_kernel(v_ref, g_ref, vp_ref, gp_ref, vn_ref, gn_ref, dw_ref, db_ref, lg_ref, lb_ref,
                 o_ref, hp_ref, acc_ref, *, tpb, ct):
    tm = v_ref.shape[0]
    ti = pl.program_id(0) % tpb
    has_prev = jnp.logical_and(ti != 0, ti != ct)
    has_next = jnp.logical_and(ti != ct - 1, ti != tpb - 1)
    glu = lambda v, g: v * jax.nn.sigmoid(g)
    hp_ref[0:CONV_HALO, :] = jnp.where(has_prev, glu(vp_ref[...], gp_ref[...]), 0.0)
    hp_ref[CONV_HALO:CONV_HALO + tm, :] = glu(v_ref[...], g_ref[...])
    hp_ref[CONV_HALO + tm:, :] = jnp.where(has_next, glu(vn_ref[...], gn_ref[...]), 0.0)

    first = CONV_HALO - CONV_K // 2
    for c in range(CONV_W // V7X_LANES):
        cs = pl.ds(c * V7X_LANES, V7X_LANES)
        acc = jnp.zeros((tm, V7X_LANES), F32)
        for k in range(CONV_K):
            acc = acc + dw_ref[k:k + 1, cs] * hp_ref[pl.ds(first + k, tm), cs]
        acc_ref[:, cs] = acc
    y = _ln(acc_ref[...] + db_ref[...]) * lg_ref[...] + lb_ref[...]
    o_ref[...] = (y * jax.nn.sigmoid(y)).astype(o_ref.dtype)


def _conv_branch(lay, z, dw, db, ln_g, ln_b):
    tm = lay.tm
    hb = tm // CONV_HALO
    n_halo = lay.n_all // CONV_HALO
    main = lambda cb: pl.BlockSpec((tm, CONV_W), lambda i: (i, cb))
    prev = lambda cb: pl.BlockSpec((CONV_HALO, CONV_W), lambda i: (jnp.maximum(i * hb - 1, 0), cb))
    nxt = lambda cb: pl.BlockSpec((CONV_HALO, CONV_W), lambda i: (jnp.minimum((i + 1) * hb, n_halo - 1), cb))
    vec = pl.BlockSpec((1, CONV_W), lambda i: (0, 0))
    kern = functools.partial(_conv_kernel, tpb=lay.tpb, ct=lay.ct)
    return pl.pallas_call(
        kern,
        grid=(lay.n_tiles,),
        in_specs=[main(ZC_CONV_V), main(ZC_CONV_G), prev(ZC_CONV_V), prev(ZC_CONV_G),
                  nxt(ZC_CONV_V), nxt(ZC_CONV_G),
                  pl.BlockSpec((CONV_K, CONV_W), lambda i: (0, 0)), vec, vec, vec],
        out_specs=pl.BlockSpec((tm, CONV_W), lambda i: (i, 0)),
        out_shape=jax.ShapeDtypeStruct((lay.n_all, CONV_W), BF16),
        scratch_shapes=[pltpu.VMEM((tm + 2 * CONV_HALO, CONV_W), F32), pltpu.VMEM((tm, CONV_W), F32)],
        compiler_params=_params(("arbitrary",), 32),
        name="conv_branch",
    )(z, z, z, z, z, z, dw, db.reshape(1, -1), ln_g.reshape(1, -1), ln_b.reshape(1, -1))


LRU_CH = V7X_LANES
LRU_PAD = V7X_SUBLANES


def _gelu_tanh(x):
    return 0.5 * x * (1.0 + jnp.tanh(math.sqrt(2.0 / math.pi) * (x + 0.044715 * (x * x * x))))


def _softplus(x):
    return jnp.maximum(x, 0.0) + jnp.log1p(jnp.exp(-jnp.abs(x)))


def _tile_scan(a, b, reverse):
    row = lax.broadcasted_iota(jnp.int32, a.shape, 1)
    for s in (1, 2, 4):
        if reverse:
            valid = row < V7X_SUBLANES - s
            shift = V7X_SUBLANES - s
        else:
            valid = row >= s
            shift = s
        a_sh = pltpu.roll(a, shift, 1)
        b_sh = pltpu.roll(b, shift, 1)
        b = jnp.where(valid, a * b_sh + b, b)
        a = jnp.where(valid, a * a_sh, a)
    return a, b


def _lru_kernel(zx_ref, zg_ref, cw_ref, cb_ref, gw_ref, gb_ref, lam_ref, o_ref,
                xp_ref, af_ref, bf_ref, ab_ref, bb_ref, *, ctx_len, chunk):
    s_all = zx_ref.shape[0]
    ch = zx_ref.shape[1]
    xp_ref[0:LRU_PAD, :] = jnp.zeros((LRU_PAD, ch), F32)
    xp_ref[LRU_PAD:LRU_PAD + s_all, :] = zx_ref[...]
    xp_ref[LRU_PAD + s_all:, :] = jnp.zeros((LRU_PAD, ch), F32)

    sp = _softplus(-lam_ref[...])
    gw = gw_ref[...]
    seg_starts = (0, ctx_len)
    seg_ends = (ctx_len, s_all)
    for r0 in range(0, s_all, chunk):
        row = r0 + lax.broadcasted_iota(jnp.int32, (chunk, ch), 0)
        xs = jnp.zeros((chunk, ch), F32) + cb_ref[...]
        for k in range(LRU_CONV_K):
            off = k - 2
            x = xp_ref[pl.ds(LRU_PAD + r0 + off, chunk), :]
            if off < 0 and r0 in seg_starts:
                x = jnp.where(row - r0 < -off, 0.0, x)
            if off > 0 and r0 + chunk in seg_ends:
                x = jnp.where(row - r0 >= chunk - off, 0.0, x)
            xs = xs + cw_ref[k:k + 1, :] * x
        pre = jnp.dot(xs.astype(BF16), gw, preferred_element_type=F32) + gb_ref[...]
        for d, (a_ref, b_ref) in enumerate(((af_ref, bf_ref), (ab_ref, bb_ref))):
            r = jax.nn.sigmoid(pre[:, (2 * d) * ch:(2 * d + 1) * ch])
            i = jax.nn.sigmoid(pre[:, (2 * d + 1) * ch:(2 * d + 2) * ch])
            a = jnp.exp(-LRU_C * r * sp[d:d + 1, :])
            b = jnp.sqrt(1.0 - a * a) * (i * xs)
            a_t, b_t = _tile_scan(a.reshape(chunk // V7X_SUBLANES, V7X_SUBLANES, ch),
                                  b.reshape(chunk // V7X_SUBLANES, V7X_SUBLANES, ch), reverse=(d == 1))
            a_ref[pl.ds(r0, chunk), :] = a_t.reshape(chunk, ch)
            b_ref[pl.ds(r0, chunk), :] = b_t.reshape(chunk, ch)

    n_t = s_all // V7X_SUBLANES
    c_t = ctx_len // V7X_SUBLANES

    def fwd_step(t, carry):
        rows = pl.ds(pl.multiple_of(t * V7X_SUBLANES, V7X_SUBLANES), V7X_SUBLANES)
        h = bf_ref[rows, :] + af_ref[rows, :] * carry
        bf_ref[rows, :] = h
        return jnp.broadcast_to(h[V7X_SUBLANES - 1:, :], h.shape)

    def bwd_step(t, carry, hi):
        t = hi - 1 - t
        rows = pl.ds(pl.multiple_of(t * V7X_SUBLANES, V7X_SUBLANES), V7X_SUBLANES)
        h = bb_ref[rows, :] + ab_ref[rows, :] * carry
        bb_ref[rows, :] = h
        return jnp.broadcast_to(h[0:1, :], h.shape)

    zero = jnp.zeros((V7X_SUBLANES, ch), F32)
    lax.fori_loop(0, n_t, fwd_step, zero)
    carry = lax.fori_loop(0, c_t, functools.partial(bwd_step, hi=c_t), zero)
    lax.fori_loop(0, n_t - c_t, functools.partial(bwd_step, hi=n_t), carry)

    for r0 in range(0, s_all, chunk):
        rows = pl.ds(r0, chunk)
        hsum = bf_ref[rows, :] + bb_ref[rows, :]
        o_ref[rows, :] = (hsum * _gelu_tanh(zg_ref[rows, :])).astype(o_ref.dtype)


def _lru_gate_matrix(gate_w):
    n_grp = LRU_W // LRU_CH
    per = LRU_CH // LRU_BS
    w = gate_w.reshape(4, n_grp, per, LRU_BS, LRU_BS)
    eye = jnp.eye(per, dtype=gate_w.dtype)
    full = jnp.einsum("gcpjk,pq->cpjgqk", w, eye)
    return full.reshape(n_grp, LRU_CH, 4 * LRU_CH)


def _lru_branch(lay, z, conv_w, conv_b, gate_w, gate_b, lam):
    s_all = lay.s_all
    n_grp = LRU_W // LRU_CH
    chunk = lay.tm
    gw = _lru_gate_matrix(gate_w).astype(BF16)
    gb = gate_b.reshape(4, n_grp, LRU_CH).transpose(1, 0, 2).reshape(n_grp, 1, 4 * LRU_CH)
    kern = functools.partial(_lru_kernel, ctx_len=lay.ctx, chunk=chunk)
    cb_x = ZC_LRU_X * (1024 // LRU_CH)
    cb_g = ZC_LRU_G * (1024 // LRU_CH)
    return pl.pallas_call(
        kern,
        grid=(lay.batch, n_grp),
        in_specs=[pl.BlockSpec((s_all, LRU_CH), lambda b, c: (b, cb_x + c)),
                  pl.BlockSpec((s_all, LRU_CH), lambda b, c: (b, cb_g + c)),
                  pl.BlockSpec((LRU_CONV_K, LRU_CH), lambda b, c: (0, c)),
                  pl.BlockSpec((1, LRU_CH), lambda b, c: (0, c)),
                  pl.BlockSpec((None, LRU_CH, 4 * LRU_CH), lambda b, c: (c, 0, 0)),
                  pl.BlockSpec((None, 1, 4 * LRU_CH), lambda b, c: (c, 0, 0)),
                  pl.BlockSpec((2, LRU_CH), lambda b, c: (0, c))],
        out_specs=pl.BlockSpec((s_all, LRU_CH), lambda b, c: (b, c)),
        out_shape=jax.ShapeDtypeStruct((lay.n_all, LRU_W), BF16),
        scratch_shapes=[pltpu.VMEM((s_all + 2 * LRU_PAD, LRU_CH), F32)]
        + [pltpu.VMEM((s_all, LRU_CH), F32) for _ in range(4)],
        compiler_params=_params(("arbitrary", "arbitrary"), 48),
        name="rglru",
    )(z, z, conv_w, conv_b.reshape(1, -1), gw, gb, lam)


def _merge_kernel(hc_ref, ha_ref, hl_ref, g0a, g0b, g1a, g1b, g2a, g2b, wc_ref, wa_ref, wl_ref, o_ref):
    half = D_MODEL // 2
    for h_ref, w_ref, ga, gb, first in ((hc_ref, wc_ref, g0a, g0b, True), (ha_ref, wa_ref, g1a, g1b, False),
                                        (hl_ref, wl_ref, g2a, g2b, False)):
        h = h_ref[...]
        for half_idx, g_ref in enumerate((ga, gb)):
            cs = slice(half_idx * half, (half_idx + 1) * half)
            p = jax.nn.sigmoid(g_ref[...]) * jnp.dot(h, w_ref[:, cs], preferred_element_type=F32)
            if first:
                o_ref[:, cs] = p
            else:
                o_ref[:, cs] = o_ref[:, cs] + p


def _merge(lay, z, hc, ha, hl, wc, wa, wl):
    tm = lay.tm
    hspec = pl.BlockSpec((tm, 1024), lambda i: (i, 0))
    gspec = lambda cb: pl.BlockSpec((tm, 1024), lambda i: (i, ZC_MERGE + cb))
    wspec = pl.BlockSpec((1024, D_MODEL), lambda i: (0, 0))
    return pl.pallas_call(
        _merge_kernel,
        grid=(lay.n_tiles,),
        in_specs=[hspec, hspec, hspec] + [gspec(cb) for cb in range(6)] + [wspec, wspec, wspec],
        out_specs=pl.BlockSpec((tm, D_MODEL), lambda i: (i, 0)),
        out_shape=jax.ShapeDtypeStruct((lay.n_all, D_MODEL), F32),
        compiler_params=_params(("arbitrary",), 56),
        name="merge",
    )(hc, ha, hl, z, z, z, z, z, z, wc, wa, wl)


def _mix_kernel(m_ref, h_ref, wo_ref, g1_ref, lg_ref, lb_ref, sh_ref, sc_ref, rw_ref, rb_ref,
                h_out, u_out, idx_out, wgt_out, rank_out, cnt_out, run_ref):
    @pl.when(pl.program_id(0) == 0)
    def _():
        run_ref[...] = jnp.zeros_like(run_ref)

    mix = jnp.dot(m_ref[...].astype(BF16), wo_ref[...], preferred_element_type=F32)
    x = _ln(DEEPNORM_ALPHA * h_ref[...] + g1_ref[...] * mix) * lg_ref[...] + lb_ref[...]
    h_out[...] = x
    u = _ln(x) * (1.0 + sc_ref[...]) + sh_ref[...]
    u_out[...] = u.astype(u_out.dtype)

    logits = jnp.dot(u, rw_ref[...], preferred_element_type=F32, precision=lax.Precision.HIGHEST) + rb_ref[...]
    tm = logits.shape[0]
    out_lane = lax.broadcasted_iota(jnp.int32, logits.shape, 1)
    lane = out_lane.astype(F32)
    idx_acc = jnp.zeros(logits.shape, F32)
    val_acc = jnp.zeros(logits.shape, F32)
    chosen = jnp.zeros(logits.shape, F32)
    hits = []
    top = None
    denom = None
    for k in range(TOP_K):
        mx = jnp.max(logits, axis=-1, keepdims=True)
        idx = jnp.min(jnp.where(logits == mx, lane, float(V7X_LANES)), axis=-1, keepdims=True)
        hit = lane == idx
        hits.append(hit)
        chosen = jnp.where(hit, 1.0, chosen)
        logits = jnp.where(hit, -jnp.inf, logits)
        if k == 0:
            top = mx
        e = jnp.exp(mx - top)
        denom = e if k == 0 else denom + e
        idx_acc = jnp.where(out_lane == k, idx, idx_acc)
        val_acc = jnp.where(out_lane == k, e, val_acc)
    idx_out[...] = idx_acc.astype(jnp.int32)
    wgt_out[...] = val_acc / denom

    earlier = (lax.broadcasted_iota(jnp.int32, (tm, tm), 0) > lax.broadcasted_iota(jnp.int32, (tm, tm), 1))
    before = jnp.dot(jnp.where(earlier, 1.0, 0.0).astype(BF16), chosen.astype(BF16),
                     preferred_element_type=F32) + run_ref[...]
    rank_acc = jnp.zeros(logits.shape, F32)
    for k in range(TOP_K):
        r = jnp.sum(jnp.where(hits[k], before, 0.0), axis=-1, keepdims=True)
        rank_acc = jnp.where(out_lane == k, r, rank_acc)
    rank_out[...] = rank_acc.astype(jnp.int32)
    run_ref[...] = run_ref[...] + jnp.sum(chosen, axis=0, keepdims=True)
    cnt_out[...] = jnp.broadcast_to(run_ref[...], cnt_out.shape)


def _mix(lay, m, h, wo, g1, ln_g, ln_b, sh2, sc2, router_w, router_b):
    tm = lay.tm
    d = D_MODEL
    row = pl.BlockSpec((tm, d), lambda i: (i, 0))
    mod = pl.BlockSpec((None, 1, d), lambda i: (lay.mod_row(i), 0, 0))
    vec = pl.BlockSpec((1, d), lambda i: (0, 0))
    small = pl.BlockSpec((tm, V7X_LANES), lambda i: (i, 0))
    pad = V7X_LANES - N_EXPERTS
    rw = jnp.pad(router_w, ((0, 0), (0, pad)))
    rb = jnp.pad(router_b, (0, pad), constant_values=-jnp.inf).reshape(1, V7X_LANES)
    return pl.pallas_call(
        _mix_kernel,
        grid=(lay.n_tiles,),
        in_specs=[row, row, pl.BlockSpec((d, d), lambda i: (0, 0)), mod, vec, vec, mod, mod,
                  pl.BlockSpec((d, V7X_LANES), lambda i: (0, 0)),
                  pl.BlockSpec((1, V7X_LANES), lambda i: (0, 0))],
        out_specs=[row, row, small, small, small, pl.BlockSpec((V7X_SUBLANES, V7X_LANES), lambda i: (0, 0))],
        out_shape=[jax.ShapeDtypeStruct((lay.n_all, d), F32),
                   jax.ShapeDtypeStruct((lay.n_all, d), BF16),
                   jax.ShapeDtypeStruct((lay.n_all, V7X_LANES), jnp.int32),
                   jax.ShapeDtypeStruct((lay.n_all, V7X_LANES), F32),
                   jax.ShapeDtypeStruct((lay.n_all, V7X_LANES), jnp.int32),
                   jax.ShapeDtypeStruct((V7X_SUBLANES, V7X_LANES), F32)],
        scratch_shapes=[pltpu.VMEM((1, V7X_LANES), F32)],
        compiler_params=_params(("arbitrary",), 56),
        name="mix_ln1_router",
    )(m, h, wo, g1, ln_g.reshape(1, d), ln_b.reshape(1, d), sh2, sc2, rw, rb)


W1_PERM = 256


def _deinterleave_matrix():
    half = W1_PERM // 2
    src = jnp.arange(W1_PERM, dtype=jnp.int32)[:, None]
    dst = jnp.arange(W1_PERM, dtype=jnp.int32)[None, :]
    want = jnp.where(dst < half, 2 * dst, 2 * (dst - half) + 1)
    return (src == want).astype(BF16)


def _w1_prep_kernel(w_ref, p_ref, g_ref, l_ref):
    half = W1_PERM // 2
    for j in range(w_ref.shape[1] // W1_PERM):
        w = w_ref[:, j * W1_PERM:(j + 1) * W1_PERM].astype(BF16)
        y = jnp.dot(w, p_ref[...], preferred_element_type=F32)
        g_ref[:, j * half:(j + 1) * half] = y[:, :half].astype(BF16)
        l_ref[:, j * half:(j + 1) * half] = y[:, half:].astype(BF16)


def _w1_prep(w1):
    g, d, f2 = w1.shape
    tc = 512
    out = jax.ShapeDtypeStruct((g, d, f2 // 2), BF16)
    return pl.pallas_call(
        _w1_prep_kernel,
        grid=(g, f2 // tc),
        in_specs=[pl.BlockSpec((None, d, tc), lambda e, j: (e, 0, j)),
                  pl.BlockSpec((W1_PERM, W1_PERM), lambda e, j: (0, 0))],
        out_specs=[pl.BlockSpec((None, d, tc // 2), lambda e, j: (e, 0, j)),
                   pl.BlockSpec((None, d, tc // 2), lambda e, j: (e, 0, j))],
        out_shape=[out, out],
        compiler_params=_params(("arbitrary", "arbitrary"), 40),
        name="w1_prep",
    )(w1, _deinterleave_matrix())


def _moe_kernel(blk_e_ref, first_ref, n_used_ref, x_ref, w1g_ref, w1l_ref, b1g_ref, b1l_ref, w2_ref, b2_ref,
                o_ref, w2bf_ref):
    del blk_e_ref
    i = pl.program_id(0)
    used = i < n_used_ref[0]

    @pl.when(jnp.logical_and(used, first_ref[i] == 1))
    def _():
        w2bf_ref[...] = w2_ref[...].astype(BF16)

    @pl.when(used)
    def _():
        x = x_ref[...]
        gl = jnp.minimum(jnp.dot(x, w1g_ref[...], preferred_element_type=F32) + b1g_ref[...], SWIGLU_LIMIT)
        lin = jnp.clip(jnp.dot(x, w1l_ref[...], preferred_element_type=F32) + b1l_ref[...],
                       -SWIGLU_LIMIT, SWIGLU_LIMIT)
        act = gl * jax.nn.sigmoid(SWIGLU_ALPHA * gl) * (lin + 1.0)
        o_ref[...] = jnp.dot(act.astype(BF16), w2bf_ref[...], preferred_element_type=F32) + b2_ref[...]

    @pl.when(jnp.logical_not(used))
    def _():
        o_ref[...] = jnp.zeros_like(o_ref)


def _moe_ffn_blocks(xs, blk_e, blk_first, n_used, w1g, w1l, b1g, b1l, w2, b2):
    n_slots, d = xs.shape
    n_blocks = n_slots // MOE_ROWS
    f = w1g.shape[2]
    wsel = lambda shape: pl.BlockSpec((None,) + shape, lambda i, be, bf, nu: (be[i], 0, 0))
    grid_spec = pltpu.PrefetchScalarGridSpec(
        num_scalar_prefetch=3,
        grid=(n_blocks,),
        in_specs=[pl.BlockSpec((MOE_ROWS, d), lambda i, be, bf, nu: (i, 0)),
                  wsel((d, f)), wsel((d, f)), wsel((1, f)), wsel((1, f)), wsel((f, d)), wsel((1, d))],
        out_specs=pl.BlockSpec((MOE_ROWS, d), lambda i, be, bf, nu: (i, 0)),
        scratch_shapes=[pltpu.VMEM((f, d), BF16)],
    )
    return pl.pallas_call(
        _moe_kernel,
        grid_spec=grid_spec,
        out_shape=jax.ShapeDtypeStruct((n_slots, d), F32),
        compiler_params=_params(("arbitrary",), 56),
        name="moe_ffn",
    )(blk_e, blk_first, n_used, xs, w1g, w1l, b1g, b1l, w2, b2)


def _moe_dispatch(top_idx, rank, counts, n_tok):
    n_assign = n_tok * TOP_K
    n_blocks = -(-n_assign // MOE_ROWS) + N_EXPERTS
    n_slots = n_blocks * MOE_ROWS
    padded = (counts + MOE_ROWS - 1) // MOE_ROWS * MOE_ROWS
    pad_end = jnp.cumsum(padded)
    pad_start = pad_end - padded
    slot_of = pad_start.at[top_idx].get(mode="promise_in_bounds") + rank
    tok = jnp.broadcast_to(jnp.arange(n_tok, dtype=jnp.int32)[:, None], (n_tok, TOP_K))
    slot_tok = jnp.zeros((n_slots,), jnp.int32).at[slot_of.reshape(-1)].set(
        tok.reshape(-1), mode="promise_in_bounds", unique_indices=True)
    blk_start = jnp.arange(n_blocks, dtype=jnp.int32) * MOE_ROWS
    blk_e = jnp.minimum(jnp.sum(blk_start[:, None] >= pad_end[None, :], axis=1), N_EXPERTS - 1).astype(jnp.int32)
    blk_first = jnp.concatenate([jnp.ones((1,), jnp.int32), (blk_e[1:] != blk_e[:-1]).astype(jnp.int32)])
    n_used = (pad_end[-1] // MOE_ROWS).astype(jnp.int32).reshape(1)
    return slot_tok, slot_of, blk_e, blk_first, n_used


def _final_kernel(h_ref, y_ref, w_ref, g2_ref, lg_ref, lb_ref, *rest, with_next):
    if with_next:
        sh_ref, sc_ref, h_out, u_out = rest
    else:
        (h_out,) = rest
    w = w_ref[...]
    y = w[:, 0:1] * y_ref[0]
    for k in range(1, TOP_K):
        y = y + w[:, k:k + 1] * y_ref[k]
    x = _ln(DEEPNORM_ALPHA * h_ref[...] + g2_ref[...] * y) * lg_ref[...] + lb_ref[...]
    h_out[...] = x
    if with_next:
        u_out[...] = (_ln(x) * (1.0 + sc_ref[...]) + sh_ref[...]).astype(u_out.dtype)


def _final(lay, h, yk, top_w, g2, ln_g, ln_b, next_mod):
    tm = lay.tm
    d = D_MODEL
    row = pl.BlockSpec((tm, d), lambda i: (i, 0))
    mod = pl.BlockSpec((None, 1, d), lambda i: (lay.mod_row(i), 0, 0))
    vec = pl.BlockSpec((1, d), lambda i: (0, 0))
    in_specs = [row, pl.BlockSpec((TOP_K, tm, d), lambda i: (0, i, 0)),
                pl.BlockSpec((tm, V7X_LANES), lambda i: (i, 0)), mod, vec, vec]
    args = [h, yk, top_w, g2, ln_g.reshape(1, d), ln_b.reshape(1, d)]
    out_specs = [row]
    out_shape = [jax.ShapeDtypeStruct((lay.n_all, d), F32)]
    if next_mod is not None:
        in_specs += [mod, mod]
        args += list(next_mod)
        out_specs.append(row)
        out_shape.append(jax.ShapeDtypeStruct((lay.n_all, d), BF16))
    outs = pl.pallas_call(
        functools.partial(_final_kernel, with_next=next_mod is not None),
        grid=(lay.n_tiles,),
        in_specs=in_specs,
        out_specs=out_specs,
        out_shape=out_shape,
        compiler_params=_params(("arbitrary",), 48),
        name="combine_ln2",
    )(*args)
    return outs if next_mod is not None else (outs[0], None)


def kernel(x, c, ctx, c_ctx, w_ada, b_ada, w_in, b_in, conv_dw, conv_db, conv_ln_g, conv_ln_b, w_conv_out,
           da_lambda, da_norm_g, w_da_out, lru_conv_w, lru_conv_b, lru_gate_w, lru_gate_b, lru_lambda,
           w_lru_out, w_out, ln1_g, ln1_b, router_w, router_b, moe_w1, moe_b1, moe_w2, moe_b2, ln2_g, ln2_b):
    batch, seq, d = x.shape
    lay = _Layout(batch, seq, ctx.shape[1])
    h = jnp.concatenate([ctx, x], axis=1).reshape(lay.n_all, d)

    c_rows = jnp.zeros((16, d), F32).at[:batch].set(c).at[batch].set(c_ctx)
    mods = []
    for l in range(DEPTH):
        ada = _ada(c_rows, w_ada[l], b_ada[l])[:batch + 1]
        mods.append([m.reshape(batch + 1, 1, d) for m in jnp.split(ada, 6, axis=-1)])

    n_exp, d_in, f2 = moe_w1.shape[1:]
    w1g, w1l = (w.reshape(DEPTH, n_exp, d_in, f2 // 2) for w in _w1_prep(moe_w1.reshape(DEPTH * n_exp, d_in, f2)))

    cos_l, sin_l = _rope_tables(lay)
    u = _ln_mod(lay, h, mods[0][0], mods[0][1])
    for l in range(DEPTH):
        sh1, sc1, g1, sh2, sc2, g2 = mods[l]
        tm_in = 512 if lay.n_all % 512 == 0 else lay.tm
        z = _mm(u, w_in[l], b_in[l], tm_in, 1024, F32)

        hc = _conv_branch(lay, z, conv_dw[l], conv_db[l], conv_ln_g[l], conv_ln_b[l])
        q, kt, v = _qkv_prep(lay, z, cos_l, sin_l)
        ha = _attention(lay, q, kt, v, da_lambda[l], da_norm_g[l], l)
        hl = _lru_branch(lay, z, lru_conv_w[l], lru_conv_b[l], lru_gate_w[l], lru_gate_b[l], lru_lambda[l])
        m = _merge(lay, z, hc, ha, hl, w_conv_out[l].astype(BF16), w_da_out[l].astype(BF16),
                   w_lru_out[l].astype(BF16))
        h, u2, top_idx, top_w, rank, counts = _mix(lay, m, h, w_out[l].astype(BF16), g1, ln1_g[l], ln1_b[l],
                                                   sh2, sc2, router_w[l], router_b[l])

        slot_tok, slot_of, blk_e, blk_first, n_used = _moe_dispatch(
            top_idx[:, :TOP_K], rank[:, :TOP_K], counts[0, :N_EXPERTS].astype(jnp.int32), lay.n_all)
        xs = u2.at[slot_tok].get(mode="promise_in_bounds")
        outs = _moe_ffn_blocks(xs, blk_e, blk_first, n_used, w1g[l], w1l[l],
                               moe_b1[l][:, None, 0::2], moe_b1[l][:, None, 1::2],
                               moe_w2[l], moe_b2[l][:, None, :])
        yk = outs.at[slot_of.T].get(mode="promise_in_bounds")
        next_mod = (mods[l + 1][0], mods[l + 1][1]) if l + 1 < DEPTH else None
        h, u = _final(lay, h, yk, top_w, g2, ln2_g[l], ln2_b[l], next_mod)

    return h.reshape(batch, lay.s_all, d)[:, lay.ctx:, :]
```

```python
import functools
import math

import jax
import jax.numpy as jnp
from jax import lax
from jax.experimental import pallas as pl
from jax.experimental.pallas import tpu as pltpu

D_MODEL = 2048
DEPTH = 2
GRID_W = 64
CONV_W = 1024
CONV_K = 31
DA_HEADS = 8
DA_DH = 64
DA_DV = 2 * DA_DH
DA_W = DA_HEADS * DA_DV
ROPE_BASE = 10000.0
LRU_W = 1024
LRU_BLOCKS = 16
LRU_BS = LRU_W // LRU_BLOCKS
LRU_CONV_K = 4
LRU_C = 8.0
N_BRANCH = 3
N_EXPERTS = 32
TOP_K = 4
D_EXPERT = 1024
SWIGLU_LIMIT = 7.0
SWIGLU_ALPHA = 1.702
LN_EPS = 1e-5
DEEPNORM_ALPHA = (2 * DEPTH) ** 0.25
IN_SPLIT = (2 * CONV_W, DA_W, DA_W, DA_W, LRU_W, LRU_W, N_BRANCH * D_MODEL)
IN_W = sum(IN_SPLIT)

V7X_LANES = 128
V7X_SUBLANES = 8
V7X_VMEM_BYTES = 64 * 1024 * 1024

ZC_CONV_V, ZC_CONV_G, ZC_Q, ZC_K, ZC_V, ZC_LRU_X, ZC_LRU_G, ZC_MERGE = 0, 1, 2, 3, 4, 5, 6, 7
CONV_HALO = 16
MOE_ROWS = 256

BF16 = jnp.bfloat16
F32 = jnp.float32


def _params(sem, vmem_mb):
    return pltpu.CompilerParams(dimension_semantics=sem, vmem_limit_bytes=vmem_mb * 1024 * 1024)


def _ln(x):
    mu = jnp.mean(x, axis=-1, keepdims=True)
    xc = x - mu
    var = jnp.mean(xc * xc, axis=-1, keepdims=True)
    return xc * lax.rsqrt(var + LN_EPS)


class _Layout:
    def __init__(self, batch, seq, ctx_len):
        self.batch, self.seq, self.ctx = batch, seq, ctx_len
        self.s_all = ctx_len + seq
        self.n_all = batch * self.s_all
        self.tm = 256 if (ctx_len % 256 == 0 and seq % 256 == 0) else 128
        assert ctx_len % self.tm == 0 and seq % self.tm == 0
        self.tpb = self.s_all // self.tm
        self.ct = ctx_len // self.tm
        self.n_tiles = self.n_all // self.tm

    def mod_row(self, i):
        return jnp.where(i % self.tpb < self.ct, self.batch, i // self.tpb)


def _ada_kernel(c_ref, w_ref, b_ref, o_ref):
    c = c_ref[...]
    a = (c * jax.nn.sigmoid(c)).astype(BF16)
    o_ref[...] = jnp.dot(a, w_ref[...].astype(BF16), preferred_element_type=F32) + b_ref[...]


def _ada(c_rows, w, b, layer):
    m, d = c_rows.shape
    n = w.shape[2]
    tn = 1024
    return pl.pallas_call(
        _ada_kernel,
        grid=(n // tn,),
        in_specs=[pl.BlockSpec((m, d), lambda j: (0, 0)),
                  pl.BlockSpec((None, d, tn), lambda j: (layer, 0, j)),
                  pl.BlockSpec((None, 1, tn), lambda j: (layer, 0, j))],
        out_specs=pl.BlockSpec((m, tn), lambda j: (0, j)),
        out_shape=jax.ShapeDtypeStruct((m, n), F32),
        compiler_params=_params(("arbitrary",), 40),
        name="ada",
    )(c_rows, w, b.reshape(b.shape[0], 1, n))


def _ln_mod_kernel(x_ref, sh_ref, sc_ref, o_ref):
    o_ref[...] = (_ln(x_ref[...]) * (1.0 + sc_ref[...]) + sh_ref[...]).astype(o_ref.dtype)


def _ln_mod(lay, h, shift, scale):
    d = h.shape[1]
    mod_spec = pl.BlockSpec((None, 1, d), lambda i: (lay.mod_row(i), 0, 0))
    return pl.pallas_call(
        _ln_mod_kernel,
        grid=(lay.n_tiles,),
        in_specs=[pl.BlockSpec((lay.tm, d), lambda i: (i, 0)), mod_spec, mod_spec],
        out_specs=pl.BlockSpec((lay.tm, d), lambda i: (i, 0)),
        out_shape=jax.ShapeDtypeStruct(h.shape, BF16),
        compiler_params=_params(("arbitrary",), 32),
        name="ln_mod",
    )(h, shift, scale)


def _mm_kernel(x_ref, w_ref, b_ref, o_ref, wbf_ref):
    @pl.when(pl.program_id(1) == 0)
    def _():
        wbf_ref[...] = w_ref[...].astype(BF16)

    acc = jnp.dot(x_ref[...], wbf_ref[...], preferred_element_type=F32)
    o_ref[...] = (acc + b_ref[...]).astype(o_ref.dtype)


def _mm(x, w, b, layer, tm, tn, out_dtype):
    m, k = x.shape
    n = w.shape[2]
    return pl.pallas_call(
        _mm_kernel,
        grid=(n // tn, m // tm),
        in_specs=[pl.BlockSpec((tm, k), lambda j, i: (i, 0)),
                  pl.BlockSpec((None, k, tn), lambda j, i: (layer, 0, j)),
                  pl.BlockSpec((None, 1, tn), lambda j, i: (layer, 0, j))],
        out_specs=pl.BlockSpec((tm, tn), lambda j, i: (i, j)),
        out_shape=jax.ShapeDtypeStruct((m, n), out_dtype),
        scratch_shapes=[pltpu.VMEM((k, tn), BF16)],
        compiler_params=_params(("arbitrary", "arbitrary"), 48),
        name="in_proj",
    )(x, w, b.reshape(b.shape[0], 1, n))


def _rope_tables(lay):
    rows = lay.seq // GRID_W
    row = jnp.repeat(jnp.arange(rows, dtype=F32), GRID_W)
    col = jnp.tile(jnp.arange(GRID_W, dtype=F32), rows)
    n_freq = DA_DH // 4
    inv_freq = ROPE_BASE ** (-jnp.arange(n_freq, dtype=F32) / n_freq)
    ang = jnp.concatenate([row[:, None] * inv_freq, col[:, None] * inv_freq], axis=-1)
    cos, sin = jnp.cos(ang), jnp.sin(ang)
    cos_l = jnp.tile(cos, (1, 4))
    sin_l = jnp.tile(jnp.concatenate([-sin, sin], axis=-1), (1, 2))
    cos_l = jnp.concatenate([jnp.ones((lay.ctx, DA_DV), F32), cos_l], axis=0)
    sin_l = jnp.concatenate([jnp.zeros((lay.ctx, DA_DV), F32), sin_l], axis=0)
    return cos_l, sin_l


def _qkv_prep_kernel(zq_ref, zk_ref, cos_ref, sin_ref, q_ref, kt_ref):
    cosf = cos_ref[...]
    sinf = sin_ref[...]
    lane = lax.broadcasted_iota(jnp.int32, cosf.shape, 1)
    first_half = (lane % DA_DH) < (DA_DH // 2)

    def rope(x):
        x = x.astype(F32)
        partner = jnp.where(first_half, pltpu.roll(x, DA_DV - DA_DH // 2, 1), pltpu.roll(x, DA_DH // 2, 1))
        return x * cosf + partner * sinf

    q_scale = (DA_DH ** -0.5) * math.log2(math.e)
    for h in range(DA_HEADS):
        sl = slice(h * DA_DV, (h + 1) * DA_DV)
        q_ref[:, sl] = (rope(zq_ref[:, sl]) * q_scale).astype(BF16)
        kt_ref[sl, :] = rope(zk_ref[:, sl]).T.astype(BF16)


def _qkv_prep(lay, z, cos_l, sin_l):
    tm = lay.tm
    zspec = lambda cb: pl.BlockSpec((tm, DA_W), lambda i: (i, cb))
    tspec = pl.BlockSpec((tm, DA_DV), lambda i: (i % lay.tpb, 0))
    return pl.pallas_call(
        _qkv_prep_kernel,
        grid=(lay.n_tiles,),
        in_specs=[zspec(ZC_Q), zspec(ZC_K), tspec, tspec],
        out_specs=[pl.BlockSpec((tm, DA_W), lambda i: (i, 0)),
                   pl.BlockSpec((None, DA_W, tm), lambda i: (i // lay.tpb, 0, i % lay.tpb))],
        out_shape=[jax.ShapeDtypeStruct((lay.n_all, DA_W), BF16),
                   jax.ShapeDtypeStruct((lay.batch, DA_W, lay.s_all), BF16)],
        compiler_params=_params(("arbitrary",), 32),
        name="qkv_prep",
    )(z, z, cos_l, sin_l)


def _attn_kernel(lam_ref, q_ref, kt_ref, v_ref, g_ref, o_ref, *, ctx_tiles, ctx_len, lam_init):
    lp = lam_ref[...]
    lam = (jnp.exp(jnp.sum(lp[0:1] * lp[1:2], axis=-1, keepdims=True))
           - jnp.exp(jnp.sum(lp[2:3] * lp[3:4], axis=-1, keepdims=True)) + lam_init)

    def attend(n_k):
        for hh in range(ATTN_HEADS_PER_STEP):
            hs = slice(hh * DA_DV, (hh + 1) * DA_DV)
            q = q_ref[:, hs]
            lane = lax.broadcasted_iota(jnp.int32, q.shape, 1)
            zero = jnp.zeros_like(q)
            kt = kt_ref[hs, :n_k]
            v = v_ref[:n_k, hs]

            def softmax_v(qm):
                s = jnp.dot(qm, kt, preferred_element_type=F32)
                p = jnp.exp2(s - jnp.max(s, axis=-1, keepdims=True))
                l = jnp.sum(p, axis=-1, keepdims=True)
                return jnp.dot(p.astype(BF16), v, preferred_element_type=F32) / l

            o = softmax_v(jnp.where(lane < DA_DH, q, zero)) - lam * softmax_v(jnp.where(lane >= DA_DH, q, zero))
            y = o * lax.rsqrt(jnp.mean(o * o, axis=-1, keepdims=True) + LN_EPS)
            o_ref[:, hs] = (y * g_ref[...] * (1.0 - lam_init)).astype(o_ref.dtype)

    is_ctx = pl.program_id(2) < ctx_tiles

    @pl.when(is_ctx)
    def _():
        attend(ctx_len)

    @pl.when(jnp.logical_not(is_ctx))
    def _():
        attend(kt_ref.shape[1])


ATTN_HEADS_PER_STEP = 4


def _attention(lay, q, kt, z, lam_p, norm_g, layer_idx):
    tq = lay.tm
    hw = ATTN_HEADS_PER_STEP * DA_DV
    lam_init = 0.8 - 0.6 * math.exp(-0.3 * layer_idx)
    kern = functools.partial(_attn_kernel, ctx_tiles=lay.ct, ctx_len=lay.ctx, lam_init=lam_init)
    v_cb = ZC_V * (1024 // hw)
    return pl.pallas_call(
        kern,
        grid=(lay.batch, DA_W // hw, lay.tpb),
        in_specs=[pl.BlockSpec((4, DA_DH), lambda b, h, i: (0, 0)),
                  pl.BlockSpec((tq, hw), lambda b, h, i: (b * lay.tpb + i, h)),
                  pl.BlockSpec((None, hw, lay.s_all), lambda b, h, i: (b, h, 0)),
                  pl.BlockSpec((lay.s_all, hw), lambda b, h, i: (b, v_cb + h)),
                  pl.BlockSpec((1, DA_DV), lambda b, h, i: (0, 0))],
        out_specs=pl.BlockSpec((tq, hw), lambda b, h, i: (b * lay.tpb + i, h)),
        out_shape=jax.ShapeDtypeStruct((lay.n_all, DA_W), BF16),
        compiler_params=_params(("arbitrary", "arbitrary", "arbitrary"), 56),
        name="diff_attn",
    )(lam_p, q, kt, z, norm_g.reshape(1, DA_DV))


def _conv_kernel(v_ref, g_ref, vp_ref, gp_ref, vn_ref, gn_ref, dw_ref, db_ref, lg_ref, lb_ref,
                 o_ref, hp_ref, acc_ref, *, tpb, ct):
    tm = v_ref.shape[0]
    ti = pl.program_id(0) % tpb
    has_prev = jnp.logical_and(ti != 0, ti != ct)
    has_next = jnp.logical_and(ti != ct - 1, ti != tpb - 1)
    glu = lambda v, g: v.astype(F32) * jax.nn.sigmoid(g.astype(F32))
    hp_ref[0:CONV_HALO, :] = jnp.where(has_prev, glu(vp_ref[...], gp_ref[...]), 0.0)
    hp_ref[CONV_HALO:CONV_HALO + tm, :] = glu(v_ref[...], g_ref[...])
    hp_ref[CONV_HALO + tm:, :] = jnp.where(has_next, glu(vn_ref[...], gn_ref[...]), 0.0)

    first = CONV_HALO - CONV_K // 2
    for c in range(CONV_W // V7X_LANES):
        cs = pl.ds(c * V7X_LANES, V7X_LANES)
        acc = jnp.zeros((tm, V7X_LANES), F32)
        for k in range(CONV_K):
            acc = acc + dw_ref[k:k + 1, cs] * hp_ref[pl.ds(first + k, tm), cs]
        acc_ref[:, cs] = acc
    y = _ln(acc_ref[...] + db_ref[...]) * lg_ref[...] + lb_ref[...]
    o_ref[...] = (y * jax.nn.sigmoid(y)).astype(o_ref.dtype)


def _conv_branch(lay, z, dw, db, ln_g, ln_b):
    tm = lay.tm
    hb = tm // CONV_HALO
    n_halo = lay.n_all // CONV_HALO
    main = lambda cb: pl.BlockSpec((tm, CONV_W), lambda i: (i, cb))
    prev = lambda cb: pl.BlockSpec((CONV_HALO, CONV_W), lambda i: (jnp.maximum(i * hb - 1, 0), cb))
    nxt = lambda cb: pl.BlockSpec((CONV_HALO, CONV_W), lambda i: (jnp.minimum((i + 1) * hb, n_halo - 1), cb))
    vec = pl.BlockSpec((1, CONV_W), lambda i: (0, 0))
    kern = functools.partial(_conv_kernel, tpb=lay.tpb, ct=lay.ct)
    return pl.pallas_call(
        kern,
        grid=(lay.n_tiles,),
        in_specs=[main(ZC_CONV_V), main(ZC_CONV_G), prev(ZC_CONV_V), prev(ZC_CONV_G),
                  nxt(ZC_CONV_V), nxt(ZC_CONV_G),
                  pl.BlockSpec((CONV_K, CONV_W), lambda i: (0, 0)), vec, vec, vec],
        out_specs=pl.BlockSpec((tm, CONV_W), lambda i: (i, 0)),
        out_shape=jax.ShapeDtypeStruct((lay.n_all, CONV_W), BF16),
        scratch_shapes=[pltpu.VMEM((tm + 2 * CONV_HALO, CONV_W), F32), pltpu.VMEM((tm, CONV_W), F32)],
        compiler_params=_params(("arbitrary",), 32),
        name="conv_branch",
    )(z, z, z, z, z, z, dw, db.reshape(1, -1), ln_g.reshape(1, -1), ln_b.reshape(1, -1))


LRU_CH = V7X_LANES
LRU_PAD = V7X_SUBLANES


def _gelu_tanh(x):
    return 0.5 * x * (1.0 + jnp.tanh(math.sqrt(2.0 / math.pi) * (x + 0.044715 * (x * x * x))))


def _softplus(x):
    return jnp.maximum(x, 0.0) + jnp.log1p(jnp.exp(-jnp.abs(x)))


def _tile_scan(a, b, reverse):
    row = lax.broadcasted_iota(jnp.int32, a.shape, 1)
    for s in (1, 2, 4):
        if reverse:
            valid = row < V7X_SUBLANES - s
            shift = V7X_SUBLANES - s
        else:
            valid = row >= s
            shift = s
        a_sh = pltpu.roll(a, shift, 1)
        b_sh = pltpu.roll(b, shift, 1)
        b = jnp.where(valid, a * b_sh + b, b)
        a = jnp.where(valid, a * a_sh, a)
    return a, b


def _lru_kernel(zx_ref, zg_ref, cw_ref, cb_ref, gw_ref, gb_ref, lam_ref, o_ref,
                xp_ref, af_ref, bf_ref, ab_ref, bb_ref, *, ctx_len, chunk):
    s_all = zx_ref.shape[0]
    ch = zx_ref.shape[1]
    xp_ref[0:LRU_PAD, :] = jnp.zeros((LRU_PAD, ch), F32)
    xp_ref[LRU_PAD:LRU_PAD + s_all, :] = zx_ref[...].astype(F32)
    xp_ref[LRU_PAD + s_all:, :] = jnp.zeros((LRU_PAD, ch), F32)

    sp = _softplus(-lam_ref[...])
    gw = gw_ref[...]
    seg_starts = (0, ctx_len)
    seg_ends = (ctx_len, s_all)
    for r0 in range(0, s_all, chunk):
        row = r0 + lax.broadcasted_iota(jnp.int32, (chunk, ch), 0)
        xs = jnp.zeros((chunk, ch), F32) + cb_ref[...]
        for k in range(LRU_CONV_K):
            off = k - 2
            x = xp_ref[pl.ds(LRU_PAD + r0 + off, chunk), :]
            if off < 0 and r0 in seg_starts:
                x = jnp.where(row - r0 < -off, 0.0, x)
            if off > 0 and r0 + chunk in seg_ends:
                x = jnp.where(row - r0 >= chunk - off, 0.0, x)
            xs = xs + cw_ref[k:k + 1, :] * x
        pre = jnp.dot(xs.astype(BF16), gw, preferred_element_type=F32) + gb_ref[...]
        for d, (a_ref, b_ref) in enumerate(((af_ref, bf_ref), (ab_ref, bb_ref))):
            r = jax.nn.sigmoid(pre[:, (2 * d) * ch:(2 * d + 1) * ch])
            i = jax.nn.sigmoid(pre[:, (2 * d + 1) * ch:(2 * d + 2) * ch])
            a = jnp.exp(-LRU_C * r * sp[d:d + 1, :])
            b = jnp.sqrt(1.0 - a * a) * (i * xs)
            a_t, b_t = _tile_scan(a.reshape(chunk // V7X_SUBLANES, V7X_SUBLANES, ch),
                                  b.reshape(chunk // V7X_SUBLANES, V7X_SUBLANES, ch), reverse=(d == 1))
            a_ref[pl.ds(r0, chunk), :] = a_t.reshape(chunk, ch)
            b_ref[pl.ds(r0, chunk), :] = b_t.reshape(chunk, ch)

    n_t = s_all // V7X_SUBLANES
    c_t = ctx_len // V7X_SUBLANES

    def fwd_step(t, carry):
        rows = pl.ds(pl.multiple_of(t * V7X_SUBLANES, V7X_SUBLANES), V7X_SUBLANES)
        h = bf_ref[rows, :] + af_ref[rows, :] * carry
        bf_ref[rows, :] = h
        return jnp.broadcast_to(h[V7X_SUBLANES - 1:, :], h.shape)

    def bwd_step(t, carry, hi):
        t = hi - 1 - t
        rows = pl.ds(pl.multiple_of(t * V7X_SUBLANES, V7X_SUBLANES), V7X_SUBLANES)
        h = bb_ref[rows, :] + ab_ref[rows, :] * carry
        bb_ref[rows, :] = h
        return jnp.broadcast_to(h[0:1, :], h.shape)

    zero = jnp.zeros((V7X_SUBLANES, ch), F32)
    lax.fori_loop(0, n_t, fwd_step, zero)
    carry = lax.fori_loop(0, c_t, functools.partial(bwd_step, hi=c_t), zero)
    lax.fori_loop(0, n_t - c_t, functools.partial(bwd_step, hi=n_t), carry)

    for r0 in range(0, s_all, chunk):
        rows = pl.ds(r0, chunk)
        hsum = bf_ref[rows, :] + bb_ref[rows, :]
        o_ref[rows, :] = (hsum * _gelu_tanh(zg_ref[rows, :].astype(F32))).astype(o_ref.dtype)


def _lru_gate_matrix(gate_w):
    n_grp = LRU_W // LRU_CH
    per = LRU_CH // LRU_BS
    w = gate_w.reshape(4, n_grp, per, LRU_BS, LRU_BS)
    eye = jnp.eye(per, dtype=gate_w.dtype)
    full = jnp.einsum("gcpjk,pq->cpjgqk", w, eye)
    return full.reshape(n_grp, LRU_CH, 4 * LRU_CH)


def _lru_branch(lay, z, conv_w, conv_b, gate_w, gate_b, lam):
    s_all = lay.s_all
    n_grp = LRU_W // LRU_CH
    chunk = lay.tm
    gw = _lru_gate_matrix(gate_w).astype(BF16)
    gb = gate_b.reshape(4, n_grp, LRU_CH).transpose(1, 0, 2).reshape(n_grp, 1, 4 * LRU_CH)
    kern = functools.partial(_lru_kernel, ctx_len=lay.ctx, chunk=chunk)
    cb_x = ZC_LRU_X * (1024 // LRU_CH)
    cb_g = ZC_LRU_G * (1024 // LRU_CH)
    return pl.pallas_call(
        kern,
        grid=(lay.batch, n_grp),
        in_specs=[pl.BlockSpec((s_all, LRU_CH), lambda b, c: (b, cb_x + c)),
                  pl.BlockSpec((s_all, LRU_CH), lambda b, c: (b, cb_g + c)),
                  pl.BlockSpec((LRU_CONV_K, LRU_CH), lambda b, c: (0, c)),
                  pl.BlockSpec((1, LRU_CH), lambda b, c: (0, c)),
                  pl.BlockSpec((None, LRU_CH, 4 * LRU_CH), lambda b, c: (c, 0, 0)),
                  pl.BlockSpec((None, 1, 4 * LRU_CH), lambda b, c: (c, 0, 0)),
                  pl.BlockSpec((2, LRU_CH), lambda b, c: (0, c))],
        out_specs=pl.BlockSpec((s_all, LRU_CH), lambda b, c: (b, c)),
        out_shape=jax.ShapeDtypeStruct((lay.n_all, LRU_W), BF16),
        scratch_shapes=[pltpu.VMEM((s_all + 2 * LRU_PAD, LRU_CH), F32)]
        + [pltpu.VMEM((s_all, LRU_CH), F32) for _ in range(4)],
        compiler_params=_params(("arbitrary", "arbitrary"), 48),
        name="rglru",
    )(z, z, conv_w, conv_b.reshape(1, -1), gw, gb, lam)


def _merge_kernel(hc_ref, ha_ref, hl_ref, g0a, g0b, g1a, g1b, g2a, g2b, wc_ref, wa_ref, wl_ref, o_ref):
    half = D_MODEL // 2
    branches = ((hc_ref, wc_ref, (g0a, g0b)), (ha_ref, wa_ref, (g1a, g1b)), (hl_ref, wl_ref, (g2a, g2b)))
    for half_idx in range(2):
        cs = slice(half_idx * half, (half_idx + 1) * half)
        acc = None
        for h_ref, w_ref, gates in branches:
            p = (jax.nn.sigmoid(gates[half_idx][...].astype(F32))
                 * jnp.dot(h_ref[...], w_ref[:, cs], preferred_element_type=F32))
            acc = p if acc is None else acc + p
        o_ref[:, cs] = acc.astype(o_ref.dtype)


def _merge(lay, z, hc, ha, hl, wc, wa, wl, layer):
    tm = lay.tm
    hspec = pl.BlockSpec((tm, 1024), lambda i: (i, 0))
    gspec = lambda cb: pl.BlockSpec((tm, 1024), lambda i: (i, ZC_MERGE + cb))
    wspec = pl.BlockSpec((None, 1024, D_MODEL), lambda i: (layer, 0, 0))
    return pl.pallas_call(
        _merge_kernel,
        grid=(lay.n_tiles,),
        in_specs=[hspec, hspec, hspec] + [gspec(cb) for cb in range(6)] + [wspec, wspec, wspec],
        out_specs=pl.BlockSpec((tm, D_MODEL), lambda i: (i, 0)),
        out_shape=jax.ShapeDtypeStruct((lay.n_all, D_MODEL), BF16),
        compiler_params=_params(("arbitrary",), 56),
        name="merge",
    )(hc, ha, hl, z, z, z, z, z, z, wc, wa, wl)


def _mix_kernel(m_ref, h_ref, wo_ref, g1_ref, lg_ref, lb_ref, sh_ref, sc_ref, rw_ref, rb_ref,
                h_out, u_out, idx_out, wgt_out, rank_out, cnt_out, run_ref):
    @pl.when(pl.program_id(0) == 0)
    def _():
        run_ref[...] = jnp.zeros_like(run_ref)

    mix = jnp.dot(m_ref[...], wo_ref[...], preferred_element_type=F32)
    x = _ln(DEEPNORM_ALPHA * h_ref[...] + g1_ref[...] * mix) * lg_ref[...] + lb_ref[...]
    h_out[...] = x
    u = _ln(x) * (1.0 + sc_ref[...]) + sh_ref[...]
    u_hi = u.astype(BF16)
    u_out[...] = u_hi

    u_lo = (u - u_hi.astype(F32)).astype(BF16)
    hi_all = jnp.dot(u_hi, rw_ref[...], preferred_element_type=F32)
    logits = (hi_all[:, :V7X_LANES] + hi_all[:, V7X_LANES:]
              + jnp.dot(u_lo, rw_ref[:, :V7X_LANES], preferred_element_type=F32) + rb_ref[...])
    tm = logits.shape[0]
    out_lane = lax.broadcasted_iota(jnp.int32, logits.shape, 1)
    lane = out_lane.astype(F32)
    idx_acc = jnp.zeros(logits.shape, F32)
    val_acc = jnp.zeros(logits.shape, F32)
    chosen = jnp.zeros(logits.shape, F32)
    hits = []
    top = None
    denom = None
    for k in range(TOP_K):
        mx = jnp.max(logits, axis=-1, keepdims=True)
        idx = jnp.min(jnp.where(logits == mx, lane, float(V7X_LANES)), axis=-1, keepdims=True)
        hit = lane == idx
        hits.append(hit)
        chosen = jnp.where(hit, 1.0, chosen)
        logits = jnp.where(hit, -jnp.inf, logits)
        if k == 0:
            top = mx
        e = jnp.exp(mx - top)
        denom = e if k == 0 else denom + e
        idx_acc = jnp.where(out_lane == k, idx, idx_acc)
        val_acc = jnp.where(out_lane == k, e, val_acc)
    idx_out[...] = idx_acc.astype(jnp.int32)
    wgt_out[...] = val_acc / denom

    earlier = (lax.broadcasted_iota(jnp.int32, (tm, tm), 0) > lax.broadcasted_iota(jnp.int32, (tm, tm), 1))
    before = jnp.dot(jnp.where(earlier, 1.0, 0.0).astype(BF16), chosen.astype(BF16),
                     preferred_element_type=F32) + run_ref[...]
    rank_acc = jnp.zeros(logits.shape, F32)
    for k in range(TOP_K):
        r = jnp.sum(jnp.where(hits[k], before, 0.0), axis=-1, keepdims=True)
        rank_acc = jnp.where(out_lane == k, r, rank_acc)
    rank_out[...] = rank_acc.astype(jnp.int32)
    run_ref[...] = run_ref[...] + jnp.sum(chosen, axis=0, keepdims=True)
    cnt_out[...] = jnp.broadcast_to(run_ref[...], cnt_out.shape)


def _mix(lay, m, h, wo, layer, g1, ln_g, ln_b, sh2, sc2, router_w, router_b):
    tm = lay.tm
    d = D_MODEL
    row = pl.BlockSpec((tm, d), lambda i: (i, 0))
    mod = pl.BlockSpec((None, 1, d), lambda i: (lay.mod_row(i), 0, 0))
    vec = pl.BlockSpec((1, d), lambda i: (0, 0))
    small = pl.BlockSpec((tm, V7X_LANES), lambda i: (i, 0))
    pad = V7X_LANES - N_EXPERTS
    rw = jnp.pad(router_w, ((0, 0), (0, pad)))
    rw_hi = rw.astype(BF16)
    rw = jnp.concatenate([rw_hi, (rw - rw_hi.astype(F32)).astype(BF16)], axis=1)
    rb = jnp.pad(router_b, (0, pad), constant_values=-jnp.inf).reshape(1, V7X_LANES)
    return pl.pallas_call(
        _mix_kernel,
        grid=(lay.n_tiles,),
        in_specs=[row, row, pl.BlockSpec((None, d, d), lambda i: (layer, 0, 0)), mod, vec, vec, mod, mod,
                  pl.BlockSpec((d, 2 * V7X_LANES), lambda i: (0, 0)),
                  pl.BlockSpec((1, V7X_LANES), lambda i: (0, 0))],
        out_specs=[row, row, small, small, small, pl.BlockSpec((V7X_SUBLANES, V7X_LANES), lambda i: (0, 0))],
        out_shape=[jax.ShapeDtypeStruct((lay.n_all, d), F32),
                   jax.ShapeDtypeStruct((lay.n_all, d), BF16),
                   jax.ShapeDtypeStruct((lay.n_all, V7X_LANES), jnp.int32),
                   jax.ShapeDtypeStruct((lay.n_all, V7X_LANES), F32),
                   jax.ShapeDtypeStruct((lay.n_all, V7X_LANES), jnp.int32),
                   jax.ShapeDtypeStruct((V7X_SUBLANES, V7X_LANES), F32)],
        scratch_shapes=[pltpu.VMEM((1, V7X_LANES), F32)],
        compiler_params=_params(("arbitrary",), 56),
        name="mix_ln1_router",
    )(m, h, wo, g1, ln_g.reshape(1, d), ln_b.reshape(1, d), sh2, sc2, rw, rb)


W1_PERM = 256


def _deinterleave_matrix():
    half = W1_PERM // 2
    src = jnp.arange(W1_PERM, dtype=jnp.int32)[:, None]
    dst = jnp.arange(W1_PERM, dtype=jnp.int32)[None, :]
    want = jnp.where(dst < half, 2 * dst, 2 * (dst - half) + 1)
    return (src == want).astype(BF16)


def _w1_prep_kernel(w_ref, p_ref, g_ref, l_ref):
    half = W1_PERM // 2
    for j in range(w_ref.shape[1] // W1_PERM):
        w = w_ref[:, j * W1_PERM:(j + 1) * W1_PERM].astype(BF16)
        y = jnp.dot(w, p_ref[...], preferred_element_type=F32)
        g_ref[:, j * half:(j + 1) * half] = y[:, :half].astype(BF16)
        l_ref[:, j * half:(j + 1) * half] = y[:, half:].astype(BF16)


def _w1_prep(w1):
    g, d, f2 = w1.shape
    tr = 256
    out = jax.ShapeDtypeStruct((g, d, f2 // 2), BF16)
    return pl.pallas_call(
        _w1_prep_kernel,
        grid=(g, d // tr),
        in_specs=[pl.BlockSpec((None, tr, f2), lambda e, j: (e, j, 0)),
                  pl.BlockSpec((W1_PERM, W1_PERM), lambda e, j: (0, 0))],
        out_specs=[pl.BlockSpec((None, tr, f2 // 2), lambda e, j: (e, j, 0)),
                   pl.BlockSpec((None, tr, f2 // 2), lambda e, j: (e, j, 0))],
        out_shape=[out, out],
        compiler_params=_params(("arbitrary", "arbitrary"), 40),
        name="w1_prep",
    )(w1, _deinterleave_matrix())


def _moe_kernel(blk_e_ref, first_ref, n_used_ref, x_ref, w1g_ref, w1l_ref, b1g_ref, b1l_ref, w2_ref, b2_ref,
                o_ref, w2bf_ref):
    del blk_e_ref
    i = pl.program_id(0)
    used = i < n_used_ref[0]

    @pl.when(jnp.logical_and(used, first_ref[i] == 1))
    def _():
        w2bf_ref[...] = w2_ref[...].astype(BF16)

    @pl.when(used)
    def _():
        x = x_ref[...]
        gl = jnp.minimum(jnp.dot(x, w1g_ref[...], preferred_element_type=F32) + b1g_ref[...], SWIGLU_LIMIT)
        lin = jnp.clip(jnp.dot(x, w1l_ref[...], preferred_element_type=F32) + b1l_ref[...],
                       -SWIGLU_LIMIT, SWIGLU_LIMIT)
        act = gl * jax.nn.sigmoid(SWIGLU_ALPHA * gl) * (lin + 1.0)
        y = jnp.dot(act.astype(BF16), w2bf_ref[...], preferred_element_type=F32) + b2_ref[...]
        o_ref[...] = y.astype(o_ref.dtype)

    @pl.when(jnp.logical_not(used))
    def _():
        o_ref[...] = jnp.zeros_like(o_ref)


def _moe_ffn_blocks(xs, blk_e, blk_first, n_used, w1g, w1l, b1g, b1l, w2, b2, layer):
    n_slots, d = xs.shape
    n_blocks = n_slots // MOE_ROWS
    f = w1g.shape[2]
    e0 = layer * N_EXPERTS
    wsel = lambda shape: pl.BlockSpec((None,) + shape, lambda i, be, bf, nu: (e0 + be[i], 0, 0))
    grid_spec = pltpu.PrefetchScalarGridSpec(
        num_scalar_prefetch=3,
        grid=(n_blocks,),
        in_specs=[pl.BlockSpec((MOE_ROWS, d), lambda i, be, bf, nu: (i, 0)),
                  wsel((d, f)), wsel((d, f)), wsel((1, f)), wsel((1, f)), wsel((f, d)), wsel((1, d))],
        out_specs=pl.BlockSpec((MOE_ROWS, d), lambda i, be, bf, nu: (i, 0)),
        scratch_shapes=[pltpu.VMEM((f, d), BF16)],
    )
    return pl.pallas_call(
        _moe_kernel,
        grid_spec=grid_spec,
        out_shape=jax.ShapeDtypeStruct((n_slots, d), BF16),
        compiler_params=_params(("arbitrary",), 56),
        name="moe_ffn",
    )(blk_e, blk_first, n_used, xs, w1g, w1l, b1g, b1l, w2, b2)


def _moe_dispatch(top_idx, rank, counts, n_tok):
    n_assign = n_tok * TOP_K
    n_blocks = -(-n_assign // MOE_ROWS) + N_EXPERTS
    n_slots = n_blocks * MOE_ROWS
    padded = (counts + MOE_ROWS - 1) // MOE_ROWS * MOE_ROWS
    pad_end = jnp.cumsum(padded)
    pad_start = pad_end - padded
    slot_of = pad_start.at[top_idx].get(mode="promise_in_bounds") + rank
    tok = jnp.broadcast_to(jnp.arange(n_tok, dtype=jnp.int32)[:, None], (n_tok, TOP_K))
    slot_tok = jnp.zeros((n_slots,), jnp.int32).at[slot_of.reshape(-1)].set(
        tok.reshape(-1), mode="promise_in_bounds", unique_indices=True)
    blk_start = jnp.arange(n_blocks, dtype=jnp.int32) * MOE_ROWS
    blk_e = jnp.minimum(jnp.sum(blk_start[:, None] >= pad_end[None, :], axis=1), N_EXPERTS - 1).astype(jnp.int32)
    blk_first = jnp.concatenate([jnp.ones((1,), jnp.int32), (blk_e[1:] != blk_e[:-1]).astype(jnp.int32)])
    n_used = (pad_end[-1] // MOE_ROWS).astype(jnp.int32).reshape(1)
    return slot_tok, slot_of, blk_e, blk_first, n_used


def _final_kernel(h_ref, y_ref, w_ref, g2_ref, lg_ref, lb_ref, *rest, with_next):
    if with_next:
        sh_ref, sc_ref, h_out, u_out = rest
    else:
        (h_out,) = rest
    w = w_ref[...]
    y = w[:, 0:1] * y_ref[0].astype(F32)
    for k in range(1, TOP_K):
        y = y + w[:, k:k + 1] * y_ref[k].astype(F32)
    x = _ln(DEEPNORM_ALPHA * h_ref[...] + g2_ref[...] * y) * lg_ref[...] + lb_ref[...]
    h_out[...] = x
    if with_next:
        u_out[...] = (_ln(x) * (1.0 + sc_ref[...]) + sh_ref[...]).astype(u_out.dtype)


def _final(lay, h, yk, top_w, g2, ln_g, ln_b, next_mod):
    tm = lay.tm
    d = D_MODEL
    row = pl.BlockSpec((tm, d), lambda i: (i, 0))
    mod = pl.BlockSpec((None, 1, d), lambda i: (lay.mod_row(i), 0, 0))
    vec = pl.BlockSpec((1, d), lambda i: (0, 0))
    in_specs = [row, pl.BlockSpec((TOP_K, tm, d), lambda i: (0, i, 0)),
                pl.BlockSpec((tm, V7X_LANES), lambda i: (i, 0)), mod, vec, vec]
    args = [h, yk, top_w, g2, ln_g.reshape(1, d), ln_b.reshape(1, d)]
    out_specs = [row]
    out_shape = [jax.ShapeDtypeStruct((lay.n_all, d), F32)]
    if next_mod is not None:
        in_specs += [mod, mod]
        args += list(next_mod)
        out_specs.append(row)
        out_shape.append(jax.ShapeDtypeStruct((lay.n_all, d), BF16))
    outs = pl.pallas_call(
        functools.partial(_final_kernel, with_next=next_mod is not None),
        grid=(lay.n_tiles,),
        in_specs=in_specs,
        out_specs=out_specs,
        out_shape=out_shape,
        compiler_params=_params(("arbitrary",), 48),
        name="combine_ln2",
    )(*args)
    return outs if next_mod is not None else (outs[0], None)


def kernel(x, c, ctx, c_ctx, w_ada, b_ada, w_in, b_in, conv_dw, conv_db, conv_ln_g, conv_ln_b, w_conv_out,
           da_lambda, da_norm_g, w_da_out, lru_conv_w, lru_conv_b, lru_gate_w, lru_gate_b, lru_lambda,
           w_lru_out, w_out, ln1_g, ln1_b, router_w, router_b, moe_w1, moe_b1, moe_w2, moe_b2, ln2_g, ln2_b):
    batch, seq, d = x.shape
    lay = _Layout(batch, seq, ctx.shape[1])
    h = jnp.concatenate([ctx, x], axis=1).reshape(lay.n_all, d)

    c_rows = jnp.zeros((16, d), F32).at[:batch].set(c).at[batch].set(c_ctx)
    mods = []
    for l in range(DEPTH):
        ada = _ada(c_rows, w_ada, b_ada, l)[:batch + 1]
        mods.append([m.reshape(batch + 1, 1, d) for m in jnp.split(ada, 6, axis=-1)])

    n_exp, d_in, f2 = moe_w1.shape[1:]
    w1g, w1l = _w1_prep(moe_w1.reshape(DEPTH * n_exp, d_in, f2))
    w2_all = moe_w2.reshape(DEPTH * n_exp, f2 // 2, d)
    b1_all = moe_b1.reshape(DEPTH * n_exp, 1, f2)
    b1g_all, b1l_all = b1_all[:, :, 0::2], b1_all[:, :, 1::2]
    b2_all = moe_b2.reshape(DEPTH * n_exp, 1, d)
    wc_bf, wa_bf, wl_bf, wo_bf = (w.astype(BF16) for w in (w_conv_out, w_da_out, w_lru_out, w_out))

    cos_l, sin_l = _rope_tables(lay)
    u = _ln_mod(lay, h, mods[0][0], mods[0][1])
    for l in range(DEPTH):
        sh1, sc1, g1, sh2, sc2, g2 = mods[l]
        tm_in = 512 if lay.n_all % 512 == 0 else lay.tm
        z = _mm(u, w_in, b_in, l, tm_in, 1024, BF16)

        hc = _conv_branch(lay, z, conv_dw[l], conv_db[l], conv_ln_g[l], conv_ln_b[l])
        q, kt = _qkv_prep(lay, z, cos_l, sin_l)
        ha = _attention(lay, q, kt, z, da_lambda[l], da_norm_g[l], l)
        hl = _lru_branch(lay, z, lru_conv_w[l], lru_conv_b[l], lru_gate_w[l], lru_gate_b[l], lru_lambda[l])
        m = _merge(lay, z, hc, ha, hl, wc_bf, wa_bf, wl_bf, l)
        h, u2, top_idx, top_w, rank, counts = _mix(lay, m, h, wo_bf, l, g1, ln1_g[l], ln1_b[l],
                                                   sh2, sc2, router_w[l], router_b[l])

        slot_tok, slot_of, blk_e, blk_first, n_used = _moe_dispatch(
            top_idx[:, :TOP_K], rank[:, :TOP_K], counts[0, :N_EXPERTS].astype(jnp.int32), lay.n_all)
        xs = u2.at[slot_tok].get(mode="promise_in_bounds")
        outs = _moe_ffn_blocks(xs, blk_e, blk_first, n_used, w1g, w1l, b1g_all, b1l_all, w2_all, b2_all, l)
        yk = outs.at[slot_of.T].get(mode="promise_in_bounds")
        next_mod = (mods[l + 1][0], mods[l + 1][1]) if l + 1 < DEPTH else None
        h, u = _final(lay, h, yk, top_w, g2, ln2_g[l], ln2_b[l], next_mod)

    return h.reshape(batch, lay.s_all, d)[:, lay.ctx:, :]
```

```python
import functools
import math

import jax
import jax.numpy as jnp
from jax import lax
from jax.experimental import pallas as pl
from jax.experimental.pallas import tpu as pltpu

D_MODEL = 2048
DEPTH = 2
GRID_W = 64
CONV_W = 1024
CONV_K = 31
DA_HEADS = 8
DA_DH = 64
DA_DV = 2 * DA_DH
DA_W = DA_HEADS * DA_DV
ROPE_BASE = 10000.0
LRU_W = 1024
LRU_BLOCKS = 16
LRU_BS = LRU_W // LRU_BLOCKS
LRU_CONV_K = 4
LRU_C = 8.0
N_BRANCH = 3
N_EXPERTS = 32
TOP_K = 4
D_EXPERT = 1024
SWIGLU_LIMIT = 7.0
SWIGLU_ALPHA = 1.702
LN_EPS = 1e-5
DEEPNORM_ALPHA = (2 * DEPTH) ** 0.25
IN_SPLIT = (2 * CONV_W, DA_W, DA_W, DA_W, LRU_W, LRU_W, N_BRANCH * D_MODEL)
IN_W = sum(IN_SPLIT)

V7X_LANES = 128
V7X_SUBLANES = 8
V7X_VMEM_BYTES = 64 * 1024 * 1024

ZC_CONV_V, ZC_CONV_G, ZC_Q, ZC_K, ZC_V, ZC_LRU_X, ZC_LRU_G, ZC_MERGE = 0, 1, 2, 3, 4, 5, 6, 7
CONV_HALO = 16
MOE_ROWS = 256

BF16 = jnp.bfloat16
F32 = jnp.float32


def _params(sem, vmem_mb):
    return pltpu.CompilerParams(dimension_semantics=sem, vmem_limit_bytes=vmem_mb * 1024 * 1024)


def _ln(x):
    mu = jnp.mean(x, axis=-1, keepdims=True)
    xc = x - mu
    var = jnp.mean(xc * xc, axis=-1, keepdims=True)
    return xc * lax.rsqrt(var + LN_EPS)


class _Layout:
    def __init__(self, batch, seq, ctx_len):
        self.batch, self.seq, self.ctx = batch, seq, ctx_len
        self.s_all = ctx_len + seq
        self.n_all = batch * self.s_all
        self.tm = 256 if (ctx_len % 256 == 0 and seq % 256 == 0) else 128
        assert ctx_len % self.tm == 0 and seq % self.tm == 0
        self.tpb = self.s_all // self.tm
        self.ct = ctx_len // self.tm
        self.n_tiles = self.n_all // self.tm

    def mod_row(self, i):
        return jnp.where(i % self.tpb < self.ct, self.batch, i // self.tpb)


def _ada_kernel(c_ref, w_ref, b_ref, o_ref):
    c = c_ref[...]
    a = (c * jax.nn.sigmoid(c)).astype(BF16)
    o_ref[...] = jnp.dot(a, w_ref[...].astype(BF16), preferred_element_type=F32) + b_ref[...]


def _ada(c_rows, w, b, layer):
    m, d = c_rows.shape
    n = w.shape[2]
    tn = 1024
    return pl.pallas_call(
        _ada_kernel,
        grid=(n // tn,),
        in_specs=[pl.BlockSpec((m, d), lambda j: (0, 0)),
                  pl.BlockSpec((None, d, tn), lambda j: (layer, 0, j)),
                  pl.BlockSpec((None, 1, tn), lambda j: (layer, 0, j))],
        out_specs=pl.BlockSpec((m, tn), lambda j: (0, j)),
        out_shape=jax.ShapeDtypeStruct((m, n), F32),
        compiler_params=_params(("arbitrary",), 40),
        name="ada",
    )(c_rows, w, b.reshape(b.shape[0], 1, n))


def _ln_mod_kernel(x_ref, sh_ref, sc_ref, o_ref):
    o_ref[...] = (_ln(x_ref[...]) * (1.0 + sc_ref[...]) + sh_ref[...]).astype(o_ref.dtype)


def _ln_mod(lay, h, shift, scale):
    d = h.shape[1]
    mod_spec = pl.BlockSpec((None, 1, d), lambda i: (lay.mod_row(i), 0, 0))
    return pl.pallas_call(
        _ln_mod_kernel,
        grid=(lay.n_tiles,),
        in_specs=[pl.BlockSpec((lay.tm, d), lambda i: (i, 0)), mod_spec, mod_spec],
        out_specs=pl.BlockSpec((lay.tm, d), lambda i: (i, 0)),
        out_shape=jax.ShapeDtypeStruct(h.shape, BF16),
        compiler_params=_params(("arbitrary",), 32),
        name="ln_mod",
    )(h, shift, scale)


def _mm_kernel(x_ref, w_ref, b_ref, o_ref, wbf_ref):
    @pl.when(pl.program_id(1) == 0)
    def _():
        wbf_ref[...] = w_ref[...].astype(BF16)

    acc = jnp.dot(x_ref[...], wbf_ref[...], preferred_element_type=F32)
    o_ref[...] = (acc + b_ref[...]).astype(o_ref.dtype)


def _mm(x, w, b, layer, tm, tn, out_dtype):
    m, k = x.shape
    n = w.shape[2]
    return pl.pallas_call(
        _mm_kernel,
        grid=(n // tn, m // tm),
        in_specs=[pl.BlockSpec((tm, k), lambda j, i: (i, 0)),
                  pl.BlockSpec((None, k, tn), lambda j, i: (layer, 0, j)),
                  pl.BlockSpec((None, 1, tn), lambda j, i: (layer, 0, j))],
        out_specs=pl.BlockSpec((tm, tn), lambda j, i: (i, j)),
        out_shape=jax.ShapeDtypeStruct((m, n), out_dtype),
        scratch_shapes=[pltpu.VMEM((k, tn), BF16)],
        compiler_params=_params(("arbitrary", "arbitrary"), 48),
        name="in_proj",
    )(x, w, b.reshape(b.shape[0], 1, n))


def _rope_tables(lay):
    rows = lay.seq // GRID_W
    row = jnp.repeat(jnp.arange(rows, dtype=F32), GRID_W)
    col = jnp.tile(jnp.arange(GRID_W, dtype=F32), rows)
    n_freq = DA_DH // 4
    inv_freq = ROPE_BASE ** (-jnp.arange(n_freq, dtype=F32) / n_freq)
    ang = jnp.concatenate([row[:, None] * inv_freq, col[:, None] * inv_freq], axis=-1)
    cos, sin = jnp.cos(ang), jnp.sin(ang)
    cos_l = jnp.tile(cos, (1, 4))
    sin_l = jnp.tile(jnp.concatenate([-sin, sin], axis=-1), (1, 2))
    cos_l = jnp.concatenate([jnp.ones((lay.ctx, DA_DV), F32), cos_l], axis=0)
    sin_l = jnp.concatenate([jnp.zeros((lay.ctx, DA_DV), F32), sin_l], axis=0)
    return cos_l, sin_l


def _qkv_prep_kernel(zq_ref, zk_ref, cos_ref, sin_ref, q_ref, kt_ref):
    cosf = cos_ref[...]
    sinf = sin_ref[...]
    lane = lax.broadcasted_iota(jnp.int32, cosf.shape, 1)
    first_half = (lane % DA_DH) < (DA_DH // 2)

    def rope(x):
        x = x.astype(F32)
        partner = jnp.where(first_half, pltpu.roll(x, DA_DV - DA_DH // 2, 1), pltpu.roll(x, DA_DH // 2, 1))
        return x * cosf + partner * sinf

    q_scale = (DA_DH ** -0.5) * math.log2(math.e)
    for h in range(DA_HEADS):
        sl = slice(h * DA_DV, (h + 1) * DA_DV)
        q_ref[:, sl] = (rope(zq_ref[:, sl]) * q_scale).astype(BF16)
        kt_ref[sl, :] = rope(zk_ref[:, sl]).T.astype(BF16)


def _qkv_prep(lay, z, cos_l, sin_l):
    tm = lay.tm
    zspec = lambda cb: pl.BlockSpec((tm, DA_W), lambda i: (i, cb))
    tspec = pl.BlockSpec((tm, DA_DV), lambda i: (i % lay.tpb, 0))
    return pl.pallas_call(
        _qkv_prep_kernel,
        grid=(lay.n_tiles,),
        in_specs=[zspec(ZC_Q), zspec(ZC_K), tspec, tspec],
        out_specs=[pl.BlockSpec((tm, DA_W), lambda i: (i, 0)),
                   pl.BlockSpec((None, DA_W, tm), lambda i: (i // lay.tpb, 0, i % lay.tpb))],
        out_shape=[jax.ShapeDtypeStruct((lay.n_all, DA_W), BF16),
                   jax.ShapeDtypeStruct((lay.batch, DA_W, lay.s_all), BF16)],
        compiler_params=_params(("arbitrary",), 32),
        name="qkv_prep",
    )(z, z, cos_l, sin_l)


def _attn_kernel(lam_ref, q_ref, kt_ref, v_ref, g_ref, o_ref, *, ctx_tiles, ctx_len, lam_init):
    lp = lam_ref[...]
    lam = (jnp.exp(jnp.sum(lp[0:1] * lp[1:2], axis=-1, keepdims=True))
           - jnp.exp(jnp.sum(lp[2:3] * lp[3:4], axis=-1, keepdims=True)) + lam_init)

    def attend(n_k):
        for hh in range(ATTN_HEADS_PER_STEP):
            hs = slice(hh * DA_DV, (hh + 1) * DA_DV)
            q = q_ref[:, hs]
            lane = lax.broadcasted_iota(jnp.int32, q.shape, 1)
            zero = jnp.zeros_like(q)
            kt = kt_ref[hs, :n_k]
            v = v_ref[:n_k, hs]

            def softmax_v(qm):
                s = jnp.dot(qm, kt, preferred_element_type=F32)
                p = jnp.exp2(s - jnp.max(s, axis=-1, keepdims=True))
                l = jnp.sum(p, axis=-1, keepdims=True)
                return jnp.dot(p.astype(BF16), v, preferred_element_type=F32) / l

            o = softmax_v(jnp.where(lane < DA_DH, q, zero)) - lam * softmax_v(jnp.where(lane >= DA_DH, q, zero))
            y = o * lax.rsqrt(jnp.mean(o * o, axis=-1, keepdims=True) + LN_EPS)
            o_ref[:, hs] = (y * g_ref[...] * (1.0 - lam_init)).astype(o_ref.dtype)

    is_ctx = pl.program_id(2) < ctx_tiles

    @pl.when(is_ctx)
    def _():
        attend(ctx_len)

    @pl.when(jnp.logical_not(is_ctx))
    def _():
        attend(kt_ref.shape[1])


ATTN_HEADS_PER_STEP = 4


def _attention(lay, q, kt, z, lam_p, norm_g, layer_idx):
    tq = lay.tm
    hw = ATTN_HEADS_PER_STEP * DA_DV
    lam_init = 0.8 - 0.6 * math.exp(-0.3 * layer_idx)
    kern = functools.partial(_attn_kernel, ctx_tiles=lay.ct, ctx_len=lay.ctx, lam_init=lam_init)
    v_cb = ZC_V * (1024 // hw)
    return pl.pallas_call(
        kern,
        grid=(lay.batch, DA_W // hw, lay.tpb),
        in_specs=[pl.BlockSpec((4, DA_DH), lambda b, h, i: (0, 0)),
                  pl.BlockSpec((tq, hw), lambda b, h, i: (b * lay.tpb + i, h)),
                  pl.BlockSpec((None, hw, lay.s_all), lambda b, h, i: (b, h, 0)),
                  pl.BlockSpec((lay.s_all, hw), lambda b, h, i: (b, v_cb + h)),
                  pl.BlockSpec((1, DA_DV), lambda b, h, i: (0, 0))],
        out_specs=pl.BlockSpec((tq, hw), lambda b, h, i: (b * lay.tpb + i, h)),
        out_shape=jax.ShapeDtypeStruct((lay.n_all, DA_W), BF16),
        compiler_params=_params(("arbitrary", "arbitrary", "arbitrary"), 56),
        name="diff_attn",
    )(lam_p, q, kt, z, norm_g.reshape(1, DA_DV))


def _conv_kernel(v_ref, g_ref, vp_ref, gp_ref, vn_ref, gn_ref, dw_ref, db_ref, lg_ref, lb_ref,
                 o_ref, hp_ref, acc_ref, *, tpb, ct):
    tm = v_ref.shape[0]
    ti = pl.program_id(0) % tpb
    has_prev = jnp.logical_and(ti != 0, ti != ct)
    has_next = jnp.logical_and(ti != ct - 1, ti != tpb - 1)
    glu = lambda v, g: v.astype(F32) * jax.nn.sigmoid(g.astype(F32))
    hp_ref[0:CONV_HALO, :] = jnp.where(has_prev, glu(vp_ref[...], gp_ref[...]), 0.0)
    hp_ref[CONV_HALO:CONV_HALO + tm, :] = glu(v_ref[...], g_ref[...])
    hp_ref[CONV_HALO + tm:, :] = jnp.where(has_next, glu(vn_ref[...], gn_ref[...]), 0.0)

    first = CONV_HALO - CONV_K // 2
    for c in range(CONV_W // V7X_LANES):
        cs = pl.ds(c * V7X_LANES, V7X_LANES)
        acc = jnp.zeros((tm, V7X_LANES), F32)
        for k in range(CONV_K):
            acc = acc + dw_ref[k:k + 1, cs] * hp_ref[pl.ds(first + k, tm), cs]
        acc_ref[:, cs] = acc
    y = _ln(acc_ref[...] + db_ref[...]) * lg_ref[...] + lb_ref[...]
    o_ref[...] = (y * jax.nn.sigmoid(y)).astype(o_ref.dtype)


def _conv_branch(lay, z, dw, db, ln_g, ln_b):
    tm = lay.tm
    hb = tm // CONV_HALO
    n_halo = lay.n_all // CONV_HALO
    main = lambda cb: pl.BlockSpec((tm, CONV_W), lambda i: (i, cb))
    prev = lambda cb: pl.BlockSpec((CONV_HALO, CONV_W), lambda i: (jnp.maximum(i * hb - 1, 0), cb))
    nxt = lambda cb: pl.BlockSpec((CONV_HALO, CONV_W), lambda i: (jnp.minimum((i + 1) * hb, n_halo - 1), cb))
    vec = pl.BlockSpec((1, CONV_W), lambda i: (0, 0))
    kern = functools.partial(_conv_kernel, tpb=lay.tpb, ct=lay.ct)
    return pl.pallas_call(
        kern,
        grid=(lay.n_tiles,),
        in_specs=[main(ZC_CONV_V), main(ZC_CONV_G), prev(ZC_CONV_V), prev(ZC_CONV_G),
                  nxt(ZC_CONV_V), nxt(ZC_CONV_G),
                  pl.BlockSpec((CONV_K, CONV_W), lambda i: (0, 0)), vec, vec, vec],
        out_specs=pl.BlockSpec((tm, CONV_W), lambda i: (i, 0)),
        out_shape=jax.ShapeDtypeStruct((lay.n_all, CONV_W), BF16),
        scratch_shapes=[pltpu.VMEM((tm + 2 * CONV_HALO, CONV_W), F32), pltpu.VMEM((tm, CONV_W), F32)],
        compiler_params=_params(("arbitrary",), 32),
        name="conv_branch",
    )(z, z, z, z, z, z, dw, db.reshape(1, -1), ln_g.reshape(1, -1), ln_b.reshape(1, -1))


LRU_CH = V7X_LANES
LRU_PAD = V7X_SUBLANES


def _gelu_tanh(x):
    return 0.5 * x * (1.0 + jnp.tanh(math.sqrt(2.0 / math.pi) * (x + 0.044715 * (x * x * x))))


def _softplus(x):
    return jnp.maximum(x, 0.0) + jnp.log1p(jnp.exp(-jnp.abs(x)))


def _tile_scan(a, b, reverse):
    row = lax.broadcasted_iota(jnp.int32, a.shape, 1)
    for s in (1, 2, 4):
        if reverse:
            valid = row < V7X_SUBLANES - s
            shift = V7X_SUBLANES - s
        else:
            valid = row >= s
            shift = s
        a_sh = pltpu.roll(a, shift, 1)
        b_sh = pltpu.roll(b, shift, 1)
        b = jnp.where(valid, a * b_sh + b, b)
        a = jnp.where(valid, a * a_sh, a)
    return a, b


def _lru_kernel(zx_ref, zg_ref, cw_ref, cb_ref, gw_ref, gb_ref, lam_ref, o_ref,
                xp_ref, af_ref, bf_ref, ab_ref, bb_ref, *, ctx_len, chunk):
    s_all = zx_ref.shape[0]
    ch = zx_ref.shape[1]
    xp_ref[0:LRU_PAD, :] = jnp.zeros((LRU_PAD, ch), F32)
    xp_ref[LRU_PAD:LRU_PAD + s_all, :] = zx_ref[...].astype(F32)
    xp_ref[LRU_PAD + s_all:, :] = jnp.zeros((LRU_PAD, ch), F32)

    sp = _softplus(-lam_ref[...])
    gw = gw_ref[...]
    seg_starts = (0, ctx_len)
    seg_ends = (ctx_len, s_all)
    for r0 in range(0, s_all, chunk):
        row = r0 + lax.broadcasted_iota(jnp.int32, (chunk, ch), 0)
        xs = jnp.zeros((chunk, ch), F32) + cb_ref[...]
        for k in range(LRU_CONV_K):
            off = k - 2
            x = xp_ref[pl.ds(LRU_PAD + r0 + off, chunk), :]
            if off < 0 and r0 in seg_starts:
                x = jnp.where(row - r0 < -off, 0.0, x)
            if off > 0 and r0 + chunk in seg_ends:
                x = jnp.where(row - r0 >= chunk - off, 0.0, x)
            xs = xs + cw_ref[k:k + 1, :] * x
        pre = jnp.dot(xs.astype(BF16), gw, preferred_element_type=F32) + gb_ref[...]
        for d, (a_ref, b_ref) in enumerate(((af_ref, bf_ref), (ab_ref, bb_ref))):
            r = jax.nn.sigmoid(pre[:, (2 * d) * ch:(2 * d + 1) * ch])
            i = jax.nn.sigmoid(pre[:, (2 * d + 1) * ch:(2 * d + 2) * ch])
            a = jnp.exp(-LRU_C * r * sp[d:d + 1, :])
            b = jnp.sqrt(1.0 - a * a) * (i * xs)
            a_t, b_t = _tile_scan(a.reshape(chunk // V7X_SUBLANES, V7X_SUBLANES, ch),
                                  b.reshape(chunk // V7X_SUBLANES, V7X_SUBLANES, ch), reverse=(d == 1))
            a_ref[pl.ds(r0, chunk), :] = a_t.reshape(chunk, ch)
            b_ref[pl.ds(r0, chunk), :] = b_t.reshape(chunk, ch)

    n_t = s_all // V7X_SUBLANES
    c_t = ctx_len // V7X_SUBLANES

    def fwd_step(t, carry):
        rows = pl.ds(pl.multiple_of(t * V7X_SUBLANES, V7X_SUBLANES), V7X_SUBLANES)
        h = bf_ref[rows, :] + af_ref[rows, :] * carry
        bf_ref[rows, :] = h
        return jnp.broadcast_to(h[V7X_SUBLANES - 1:, :], h.shape)

    def bwd_step(t, carry, hi):
        t = hi - 1 - t
        rows = pl.ds(pl.multiple_of(t * V7X_SUBLANES, V7X_SUBLANES), V7X_SUBLANES)
        h = bb_ref[rows, :] + ab_ref[rows, :] * carry
        bb_ref[rows, :] = h
        return jnp.broadcast_to(h[0:1, :], h.shape)

    zero = jnp.zeros((V7X_SUBLANES, ch), F32)
    lax.fori_loop(0, n_t, fwd_step, zero)
    carry = lax.fori_loop(0, c_t, functools.partial(bwd_step, hi=c_t), zero)
    lax.fori_loop(0, n_t - c_t, functools.partial(bwd_step, hi=n_t), carry)

    for r0 in range(0, s_all, chunk):
        rows = pl.ds(r0, chunk)
        hsum = bf_ref[rows, :] + bb_ref[rows, :]
        o_ref[rows, :] = (hsum * _gelu_tanh(zg_ref[rows, :].astype(F32))).astype(o_ref.dtype)


def _lru_gate_matrix(gate_w):
    n_grp = LRU_W // LRU_CH
    per = LRU_CH // LRU_BS
    w = gate_w.reshape(4, n_grp, per, LRU_BS, LRU_BS)
    eye = jnp.eye(per, dtype=gate_w.dtype)
    full = jnp.einsum("gcpjk,pq->cpjgqk", w, eye)
    return full.reshape(n_grp, LRU_CH, 4 * LRU_CH)


def _lru_branch(lay, z, conv_w, conv_b, gate_w, gate_b, lam):
    s_all = lay.s_all
    n_grp = LRU_W // LRU_CH
    chunk = lay.tm
    gw = _lru_gate_matrix(gate_w).astype(BF16)
    gb = gate_b.reshape(4, n_grp, LRU_CH).transpose(1, 0, 2).reshape(n_grp, 1, 4 * LRU_CH)
    kern = functools.partial(_lru_kernel, ctx_len=lay.ctx, chunk=chunk)
    cb_x = ZC_LRU_X * (1024 // LRU_CH)
    cb_g = ZC_LRU_G * (1024 // LRU_CH)
    return pl.pallas_call(
        kern,
        grid=(lay.batch, n_grp),
        in_specs=[pl.BlockSpec((s_all, LRU_CH), lambda b, c: (b, cb_x + c)),
                  pl.BlockSpec((s_all, LRU_CH), lambda b, c: (b, cb_g + c)),
                  pl.BlockSpec((LRU_CONV_K, LRU_CH), lambda b, c: (0, c)),
                  pl.BlockSpec((1, LRU_CH), lambda b, c: (0, c)),
                  pl.BlockSpec((None, LRU_CH, 4 * LRU_CH), lambda b, c: (c, 0, 0)),
                  pl.BlockSpec((None, 1, 4 * LRU_CH), lambda b, c: (c, 0, 0)),
                  pl.BlockSpec((2, LRU_CH), lambda b, c: (0, c))],
        out_specs=pl.BlockSpec((s_all, LRU_CH), lambda b, c: (b, c)),
        out_shape=jax.ShapeDtypeStruct((lay.n_all, LRU_W), BF16),
        scratch_shapes=[pltpu.VMEM((s_all + 2 * LRU_PAD, LRU_CH), F32)]
        + [pltpu.VMEM((s_all, LRU_CH), F32) for _ in range(4)],
        compiler_params=_params(("arbitrary", "arbitrary"), 48),
        name="rglru",
    )(z, z, conv_w, conv_b.reshape(1, -1), gw, gb, lam)


def _merge_kernel(hc_ref, ha_ref, hl_ref, g0a, g0b, g1a, g1b, g2a, g2b, wc_ref, wa_ref, wl_ref, o_ref):
    half = D_MODEL // 2
    branches = ((hc_ref, wc_ref, (g0a, g0b)), (ha_ref, wa_ref, (g1a, g1b)), (hl_ref, wl_ref, (g2a, g2b)))
    for half_idx in range(2):
        cs = slice(half_idx * half, (half_idx + 1) * half)
        acc = None
        for h_ref, w_ref, gates in branches:
            p = (jax.nn.sigmoid(gates[half_idx][...].astype(F32))
                 * jnp.dot(h_ref[...], w_ref[:, cs], preferred_element_type=F32))
            acc = p if acc is None else acc + p
        o_ref[:, cs] = acc.astype(o_ref.dtype)


def _merge(lay, z, hc, ha, hl, wc, wa, wl, layer):
    tm = lay.tm
    hspec = pl.BlockSpec((tm, 1024), lambda i: (i, 0))
    gspec = lambda cb: pl.BlockSpec((tm, 1024), lambda i: (i, ZC_MERGE + cb))
    wspec = pl.BlockSpec((None, 1024, D_MODEL), lambda i: (layer, 0, 0))
    return pl.pallas_call(
        _merge_kernel,
        grid=(lay.n_tiles,),
        in_specs=[hspec, hspec, hspec] + [gspec(cb) for cb in range(6)] + [wspec, wspec, wspec],
        out_specs=pl.BlockSpec((tm, D_MODEL), lambda i: (i, 0)),
        out_shape=jax.ShapeDtypeStruct((lay.n_all, D_MODEL), BF16),
        compiler_params=_params(("arbitrary",), 56),
        name="merge",
    )(hc, ha, hl, z, z, z, z, z, z, wc, wa, wl)


def _mix_kernel(m_ref, h_ref, wo_ref, g1_ref, lg_ref, lb_ref, sh_ref, sc_ref, rw_ref, rb_ref,
                h_out, u_out, idx_out, wgt_out, rank_out, cnt_out, run_ref):
    @pl.when(pl.program_id(0) == 0)
    def _():
        run_ref[...] = jnp.zeros_like(run_ref)

    mix = jnp.dot(m_ref[...], wo_ref[...], preferred_element_type=F32)
    x = _ln(DEEPNORM_ALPHA * h_ref[...] + g1_ref[...] * mix) * lg_ref[...] + lb_ref[...]
    h_out[...] = x
    u = _ln(x) * (1.0 + sc_ref[...]) + sh_ref[...]
    u_hi = u.astype(BF16)
    u_out[...] = u_hi

    u_lo = (u - u_hi.astype(F32)).astype(BF16)
    hi_all = jnp.dot(u_hi, rw_ref[...], preferred_element_type=F32)
    logits = (hi_all[:, :V7X_LANES] + hi_all[:, V7X_LANES:]
              + jnp.dot(u_lo, rw_ref[:, :V7X_LANES], preferred_element_type=F32) + rb_ref[...])
    tm = logits.shape[0]
    out_lane = lax.broadcasted_iota(jnp.int32, logits.shape, 1)
    lane = out_lane.astype(F32)
    idx_acc = jnp.zeros(logits.shape, F32)
    val_acc = jnp.zeros(logits.shape, F32)
    chosen = jnp.zeros(logits.shape, F32)
    hits = []
    top = None
    denom = None
    for k in range(TOP_K):
        mx = jnp.max(logits, axis=-1, keepdims=True)
        idx = jnp.min(jnp.where(logits == mx, lane, float(V7X_LANES)), axis=-1, keepdims=True)
        hit = lane == idx
        hits.append(hit)
        chosen = jnp.where(hit, 1.0, chosen)
        logits = jnp.where(hit, -jnp.inf, logits)
        if k == 0:
            top = mx
        e = jnp.exp(mx - top)
        denom = e if k == 0 else denom + e
        idx_acc = jnp.where(out_lane == k, idx, idx_acc)
        val_acc = jnp.where(out_lane == k, e, val_acc)
    idx_out[...] = idx_acc.astype(jnp.int32)
    wgt_out[...] = val_acc / denom

    earlier = (lax.broadcasted_iota(jnp.int32, (tm, tm), 0) > lax.broadcasted_iota(jnp.int32, (tm, tm), 1))
    before = jnp.dot(jnp.where(earlier, 1.0, 0.0).astype(BF16), chosen.astype(BF16),
                     preferred_element_type=F32) + run_ref[...]
    rank_acc = jnp.zeros(logits.shape, F32)
    for k in range(TOP_K):
        r = jnp.sum(jnp.where(hits[k], before, 0.0), axis=-1, keepdims=True)
        rank_acc = jnp.where(out_lane == k, r, rank_acc)
    rank_out[...] = rank_acc.astype(jnp.int32)
    run_ref[...] = run_ref[...] + jnp.sum(chosen, axis=0, keepdims=True)
    cnt_out[...] = jnp.broadcast_to(run_ref[...], cnt_out.shape)


def _mix(lay, m, h, wo, layer, g1, ln_g, ln_b, sh2, sc2, router_w, router_b):
    tm = lay.tm
    d = D_MODEL
    row = pl.BlockSpec((tm, d), lambda i: (i, 0))
    mod = pl.BlockSpec((None, 1, d), lambda i: (lay.mod_row(i), 0, 0))
    vec = pl.BlockSpec((1, d), lambda i: (0, 0))
    small = pl.BlockSpec((tm, V7X_LANES), lambda i: (i, 0))
    pad = V7X_LANES - N_EXPERTS
    rw = jnp.pad(router_w, ((0, 0), (0, pad)))
    rw_hi = rw.astype(BF16)
    rw = jnp.concatenate([rw_hi, (rw - rw_hi.astype(F32)).astype(BF16)], axis=1)
    rb = jnp.pad(router_b, (0, pad), constant_values=-jnp.inf).reshape(1, V7X_LANES)
    return pl.pallas_call(
        _mix_kernel,
        grid=(lay.n_tiles,),
        in_specs=[row, row, pl.BlockSpec((None, d, d), lambda i: (layer, 0, 0)), mod, vec, vec, mod, mod,
                  pl.BlockSpec((d, 2 * V7X_LANES), lambda i: (0, 0)),
                  pl.BlockSpec((1, V7X_LANES), lambda i: (0, 0))],
        out_specs=[row, row, small, small, small, pl.BlockSpec((V7X_SUBLANES, V7X_LANES), lambda i: (0, 0))],
        out_shape=[jax.ShapeDtypeStruct((lay.n_all, d), F32),
                   jax.ShapeDtypeStruct((lay.n_all, d), BF16),
                   jax.ShapeDtypeStruct((lay.n_all, V7X_LANES), jnp.int32),
                   jax.ShapeDtypeStruct((lay.n_all, V7X_LANES), F32),
                   jax.ShapeDtypeStruct((lay.n_all, V7X_LANES), jnp.int32),
                   jax.ShapeDtypeStruct((V7X_SUBLANES, V7X_LANES), F32)],
        scratch_shapes=[pltpu.VMEM((1, V7X_LANES), F32)],
        compiler_params=_params(("arbitrary",), 56),
        name="mix_ln1_router",
    )(m, h, wo, g1, ln_g.reshape(1, d), ln_b.reshape(1, d), sh2, sc2, rw, rb)


W1_PERM = 256


def _deinterleave_matrix():
    half = W1_PERM // 2
    src = jnp.arange(W1_PERM, dtype=jnp.int32)[:, None]
    dst = jnp.arange(W1_PERM, dtype=jnp.int32)[None, :]
    want = jnp.where(dst < half, 2 * dst, 2 * (dst - half) + 1)
    return (src == want).astype(BF16)


def _moe_kernel(blk_e_ref, first_ref, next_ref, n_used_ref,
                x_ref, p_ref, b1g_ref, b1l_ref, b2_ref, w1_hbm, w2_hbm,
                o_ref, st1_ref, st2_ref, w1g_ref, w1l_ref, w2bf_ref, sem, *, e0):
    i = pl.program_id(0)
    used = i < n_used_ref[0]

    def weight_copies(e):
        return (pltpu.make_async_copy(w1_hbm.at[e], st1_ref, sem.at[0]),
                pltpu.make_async_copy(w2_hbm.at[e], st2_ref, sem.at[1]))

    @pl.when(jnp.logical_and(used, first_ref[i] == 1))
    def _():
        @pl.when(i == 0)
        def _():
            for cp in weight_copies(e0 + blk_e_ref[0]):
                cp.start()

        for cp in weight_copies(e0 + blk_e_ref[i]):
            cp.wait()

        half = W1_PERM // 2

        def slab(r, carry):
            rows = pl.ds(pl.multiple_of(r * W1_PERM, W1_PERM), W1_PERM)
            for j in range(st1_ref.shape[1] // W1_PERM):
                w = st1_ref[rows, j * W1_PERM:(j + 1) * W1_PERM].astype(BF16)
                y = jnp.dot(w, p_ref[...], preferred_element_type=F32)
                w1g_ref[rows, j * half:(j + 1) * half] = y[:, :half].astype(BF16)
                w1l_ref[rows, j * half:(j + 1) * half] = y[:, half:].astype(BF16)
            return carry

        lax.fori_loop(0, st1_ref.shape[0] // W1_PERM, slab, 0)
        w2bf_ref[...] = st2_ref[...].astype(BF16)

        @pl.when(next_ref[i] >= 0)
        def _():
            for cp in weight_copies(e0 + next_ref[i]):
                cp.start()

    @pl.when(used)
    def _():
        x = x_ref[...]
        gl = jnp.minimum(jnp.dot(x, w1g_ref[...], preferred_element_type=F32) + b1g_ref[...], SWIGLU_LIMIT)
        lin = jnp.clip(jnp.dot(x, w1l_ref[...], preferred_element_type=F32) + b1l_ref[...],
                       -SWIGLU_LIMIT, SWIGLU_LIMIT)
        act = gl * jax.nn.sigmoid(SWIGLU_ALPHA * gl) * (lin + 1.0)
        y = jnp.dot(act.astype(BF16), w2bf_ref[...], preferred_element_type=F32) + b2_ref[...]
        o_ref[...] = y.astype(o_ref.dtype)

    @pl.when(jnp.logical_not(used))
    def _():
        o_ref[...] = jnp.zeros_like(o_ref)


def _moe_ffn_blocks(xs, blk_e, blk_first, blk_next, n_used, w1, w2, b1g, b1l, b2, layer):
    n_slots, d = xs.shape
    n_blocks = n_slots // MOE_ROWS
    f = w2.shape[1]
    e0 = layer * N_EXPERTS
    bsel = lambda width: pl.BlockSpec((None, 1, width), lambda i, be, bf, bn, nu: (e0 + be[i], 0, 0))
    grid_spec = pltpu.PrefetchScalarGridSpec(
        num_scalar_prefetch=4,
        grid=(n_blocks,),
        in_specs=[pl.BlockSpec((MOE_ROWS, d), lambda i, be, bf, bn, nu: (i, 0)),
                  pl.BlockSpec((W1_PERM, W1_PERM), lambda i, be, bf, bn, nu: (0, 0)),
                  bsel(f), bsel(f), bsel(d),
                  pl.BlockSpec(memory_space=pl.ANY), pl.BlockSpec(memory_space=pl.ANY)],
        out_specs=pl.BlockSpec((MOE_ROWS, d), lambda i, be, bf, bn, nu: (i, 0)),
        scratch_shapes=[pltpu.VMEM((d, 2 * f), F32), pltpu.VMEM((f, d), F32),
                        pltpu.VMEM((d, f), BF16), pltpu.VMEM((d, f), BF16), pltpu.VMEM((f, d), BF16),
                        pltpu.SemaphoreType.DMA((2,))],
    )
    return pl.pallas_call(
        functools.partial(_moe_kernel, e0=e0),
        grid_spec=grid_spec,
        out_shape=jax.ShapeDtypeStruct((n_slots, d), BF16),
        compiler_params=_params(("arbitrary",), 58),
        name="moe_ffn",
    )(blk_e, blk_first, blk_next, n_used, xs, _deinterleave_matrix(), b1g, b1l, b2, w1, w2)


def _moe_dispatch(top_idx, rank, counts, n_tok):
    n_assign = n_tok * TOP_K
    n_blocks = -(-n_assign // MOE_ROWS) + N_EXPERTS
    n_slots = n_blocks * MOE_ROWS
    padded = (counts + MOE_ROWS - 1) // MOE_ROWS * MOE_ROWS
    pad_end = jnp.cumsum(padded)
    pad_start = pad_end - padded
    slot_of = pad_start.at[top_idx].get(mode="promise_in_bounds") + rank
    tok = jnp.broadcast_to(jnp.arange(n_tok, dtype=jnp.int32)[:, None], (n_tok, TOP_K))
    slot_tok = jnp.zeros((n_slots,), jnp.int32).at[slot_of.reshape(-1)].set(
        tok.reshape(-1), mode="promise_in_bounds", unique_indices=True)
    blk_idx = jnp.arange(n_blocks, dtype=jnp.int32)
    blk_e = jnp.minimum(jnp.sum((blk_idx * MOE_ROWS)[:, None] >= pad_end[None, :], axis=1),
                        N_EXPERTS - 1).astype(jnp.int32)
    blk_first = jnp.concatenate([jnp.ones((1,), jnp.int32), (blk_e[1:] != blk_e[:-1]).astype(jnp.int32)])
    n_used = (pad_end[-1] // MOE_ROWS).astype(jnp.int32)
    starts = jnp.where(jnp.logical_and(blk_first == 1, blk_idx < n_used), blk_idx, n_blocks)
    nxt = jnp.concatenate([lax.cummin(starts, axis=0, reverse=True)[1:], jnp.full((1,), n_blocks, jnp.int32)])
    blk_next = jnp.where(nxt < n_blocks, blk_e.at[jnp.minimum(nxt, n_blocks - 1)].get(mode="promise_in_bounds"),
                         -1).astype(jnp.int32)
    return slot_tok, slot_of, blk_e, blk_first, blk_next, n_used.reshape(1)


def _final_kernel(h_ref, y_ref, w_ref, g2_ref, lg_ref, lb_ref, *rest, with_next):
    if with_next:
        sh_ref, sc_ref, h_out, u_out = rest
    else:
        (h_out,) = rest
    w = w_ref[...]
    y = w[:, 0:1] * y_ref[0].astype(F32)
    for k in range(1, TOP_K):
        y = y + w[:, k:k + 1] * y_ref[k].astype(F32)
    x = _ln(DEEPNORM_ALPHA * h_ref[...] + g2_ref[...] * y) * lg_ref[...] + lb_ref[...]
    h_out[...] = x
    if with_next:
        u_out[...] = (_ln(x) * (1.0 + sc_ref[...]) + sh_ref[...]).astype(u_out.dtype)


def _final(lay, h, yk, top_w, g2, ln_g, ln_b, next_mod):
    tm = lay.tm
    d = D_MODEL
    row = pl.BlockSpec((tm, d), lambda i: (i, 0))
    mod = pl.BlockSpec((None, 1, d), lambda i: (lay.mod_row(i), 0, 0))
    vec = pl.BlockSpec((1, d), lambda i: (0, 0))
    in_specs = [row, pl.BlockSpec((TOP_K, tm, d), lambda i: (0, i, 0)),
                pl.BlockSpec((tm, V7X_LANES), lambda i: (i, 0)), mod, vec, vec]
    args = [h, yk, top_w, g2, ln_g.reshape(1, d), ln_b.reshape(1, d)]
    out_specs = [row]
    out_shape = [jax.ShapeDtypeStruct((lay.n_all, d), F32)]
    if next_mod is not None:
        in_specs += [mod, mod]
        args += list(next_mod)
        out_specs.append(row)
        out_shape.append(jax.ShapeDtypeStruct((lay.n_all, d), BF16))
    outs = pl.pallas_call(
        functools.partial(_final_kernel, with_next=next_mod is not None),
        grid=(lay.n_tiles,),
        in_specs=in_specs,
        out_specs=out_specs,
        out_shape=out_shape,
        compiler_params=_params(("arbitrary",), 48),
        name="combine_ln2",
    )(*args)
    return outs if next_mod is not None else (outs[0], None)


def kernel(x, c, ctx, c_ctx, w_ada, b_ada, w_in, b_in, conv_dw, conv_db, conv_ln_g, conv_ln_b, w_conv_out,
           da_lambda, da_norm_g, w_da_out, lru_conv_w, lru_conv_b, lru_gate_w, lru_gate_b, lru_lambda,
           w_lru_out, w_out, ln1_g, ln1_b, router_w, router_b, moe_w1, moe_b1, moe_w2, moe_b2, ln2_g, ln2_b):
    batch, seq, d = x.shape
    lay = _Layout(batch, seq, ctx.shape[1])
    h = jnp.concatenate([ctx, x], axis=1).reshape(lay.n_all, d)

    c_rows = jnp.zeros((16, d), F32).at[:batch].set(c).at[batch].set(c_ctx)
    mods = []
    for l in range(DEPTH):
        ada = _ada(c_rows, w_ada, b_ada, l)[:batch + 1]
        mods.append([m.reshape(batch + 1, 1, d) for m in jnp.split(ada, 6, axis=-1)])

    n_exp, d_in, f2 = moe_w1.shape[1:]
    w1_all = moe_w1.reshape(DEPTH * n_exp, d_in, f2)
    w2_all = moe_w2.reshape(DEPTH * n_exp, f2 // 2, d)
    b1_all = moe_b1.reshape(DEPTH * n_exp, 1, f2)
    b1g_all, b1l_all = b1_all[:, :, 0::2], b1_all[:, :, 1::2]
    b2_all = moe_b2.reshape(DEPTH * n_exp, 1, d)
    wc_bf, wa_bf, wl_bf, wo_bf = (w.astype(BF16) for w in (w_conv_out, w_da_out, w_lru_out, w_out))

    cos_l, sin_l = _rope_tables(lay)
    u = _ln_mod(lay, h, mods[0][0], mods[0][1])
    for l in range(DEPTH):
        sh1, sc1, g1, sh2, sc2, g2 = mods[l]
        tm_in = 512 if lay.n_all % 512 == 0 else lay.tm
        z = _mm(u, w_in, b_in, l, tm_in, 1024, BF16)

        hc = _conv_branch(lay, z, conv_dw[l], conv_db[l], conv_ln_g[l], conv_ln_b[l])
        q, kt = _qkv_prep(lay, z, cos_l, sin_l)
        ha = _attention(lay, q, kt, z, da_lambda[l], da_norm_g[l], l)
        hl = _lru_branch(lay, z, lru_conv_w[l], lru_conv_b[l], lru_gate_w[l], lru_gate_b[l], lru_lambda[l])
        m = _merge(lay, z, hc, ha, hl, wc_bf, wa_bf, wl_bf, l)
        h, u2, top_idx, top_w, rank, counts = _mix(lay, m, h, wo_bf, l, g1, ln1_g[l], ln1_b[l],
                                                   sh2, sc2, router_w[l], router_b[l])

        slot_tok, slot_of, blk_e, blk_first, blk_next, n_used = _moe_dispatch(
            top_idx[:, :TOP_K], rank[:, :TOP_K], counts[0, :N_EXPERTS].astype(jnp.int32), lay.n_all)
        xs = u2.at[slot_tok].get(mode="promise_in_bounds")
        outs = _moe_ffn_blocks(xs, blk_e, blk_first, blk_next, n_used, w1_all, w2_all,
                               b1g_all, b1l_all, b2_all, l)
        yk = outs.at[slot_of.T].get(mode="promise_in_bounds")
        next_mod = (mods[l + 1][0], mods[l + 1][1]) if l + 1 < DEPTH else None
        h, u = _final(lay, h, yk, top_w, g2, ln2_g[l], ln2_b[l], next_mod)

    return h.reshape(batch, lay.s_all, d)[:, lay.ctx:, :]
```

```python
import functools
import math

import jax
import jax.numpy as jnp
from jax import lax
from jax.experimental import pallas as pl
from jax.experimental.pallas import tpu as pltpu

D_MODEL = 2048
DEPTH = 2
GRID_W = 64
CONV_W = 1024
CONV_K = 31
DA_HEADS = 8
DA_DH = 64
DA_DV = 2 * DA_DH
DA_W = DA_HEADS * DA_DV
ROPE_BASE = 10000.0
LRU_W = 1024
LRU_BLOCKS = 16
LRU_BS = LRU_W // LRU_BLOCKS
LRU_CONV_K = 4
LRU_C = 8.0
N_BRANCH = 3
N_EXPERTS = 32
TOP_K = 4
D_EXPERT = 1024
SWIGLU_LIMIT = 7.0
SWIGLU_ALPHA = 1.702
LN_EPS = 1e-5
DEEPNORM_ALPHA = (2 * DEPTH) ** 0.25
IN_SPLIT = (2 * CONV_W, DA_W, DA_W, DA_W, LRU_W, LRU_W, N_BRANCH * D_MODEL)
IN_W = sum(IN_SPLIT)

V7X_LANES = 128
V7X_SUBLANES = 8
V7X_VMEM_BYTES = 64 * 1024 * 1024

ZC_CONV_V, ZC_CONV_G, ZC_Q, ZC_K, ZC_V, ZC_LRU_X, ZC_LRU_G, ZC_MERGE = 0, 1, 2, 3, 4, 5, 6, 7
CONV_HALO = 16
MOE_ROWS = 256

BF16 = jnp.bfloat16
F32 = jnp.float32


def _params(sem, vmem_mb):
    return pltpu.CompilerParams(dimension_semantics=sem, vmem_limit_bytes=vmem_mb * 1024 * 1024)


def _ln(x):
    mu = jnp.mean(x, axis=-1, keepdims=True)
    xc = x - mu
    var = jnp.mean(xc * xc, axis=-1, keepdims=True)
    return xc * lax.rsqrt(var + LN_EPS)


class _Layout:
    def __init__(self, batch, seq, ctx_len):
        self.batch, self.seq, self.ctx = batch, seq, ctx_len
        self.s_all = ctx_len + seq
        self.n_all = batch * self.s_all
        self.tm = 256 if (ctx_len % 256 == 0 and seq % 256 == 0) else 128
        assert ctx_len % self.tm == 0 and seq % self.tm == 0
        self.tpb = self.s_all // self.tm
        self.ct = ctx_len // self.tm
        self.n_tiles = self.n_all // self.tm

    def mod_row(self, i):
        return jnp.where(i % self.tpb < self.ct, self.batch, i // self.tpb)


def _ada_kernel(c_ref, w_ref, b_ref, o_ref):
    c = c_ref[...]
    a = (c * jax.nn.sigmoid(c)).astype(BF16)
    o_ref[...] = jnp.dot(a, w_ref[...].astype(BF16), preferred_element_type=F32) + b_ref[...]


def _ada(c_rows, w, b, layer):
    m, d = c_rows.shape
    n = w.shape[2]
    tn = 1024
    return pl.pallas_call(
        _ada_kernel,
        grid=(n // tn,),
        in_specs=[pl.BlockSpec((m, d), lambda j: (0, 0)),
                  pl.BlockSpec((None, d, tn), lambda j: (layer, 0, j)),
                  pl.BlockSpec((None, 1, tn), lambda j: (layer, 0, j))],
        out_specs=pl.BlockSpec((m, tn), lambda j: (0, j)),
        out_shape=jax.ShapeDtypeStruct((m, n), F32),
        compiler_params=_params(("arbitrary",), 40),
        name="ada",
    )(c_rows, w, b.reshape(b.shape[0], 1, n))


def _ln_mod_kernel(x_ref, sh_ref, sc_ref, o_ref):
    o_ref[...] = (_ln(x_ref[...]) * (1.0 + sc_ref[...]) + sh_ref[...]).astype(o_ref.dtype)


def _ln_mod(lay, h, shift, scale):
    d = h.shape[1]
    mod_spec = pl.BlockSpec((None, 1, d), lambda i: (lay.mod_row(i), 0, 0))
    return pl.pallas_call(
        _ln_mod_kernel,
        grid=(lay.n_tiles,),
        in_specs=[pl.BlockSpec((lay.tm, d), lambda i: (i, 0)), mod_spec, mod_spec],
        out_specs=pl.BlockSpec((lay.tm, d), lambda i: (i, 0)),
        out_shape=jax.ShapeDtypeStruct(h.shape, BF16),
        compiler_params=_params(("arbitrary",), 32),
        name="ln_mod",
    )(h, shift, scale)


def _mm_kernel(x_ref, w_ref, b_ref, o_ref, wbf_ref):
    @pl.when(pl.program_id(1) == 0)
    def _():
        wbf_ref[...] = w_ref[...].astype(BF16)

    acc = jnp.dot(x_ref[...], wbf_ref[...], preferred_element_type=F32)
    o_ref[...] = (acc + b_ref[...]).astype(o_ref.dtype)


def _mm(x, w, b, layer, tm, tn, out_dtype):
    m, k = x.shape
    n = w.shape[2]
    return pl.pallas_call(
        _mm_kernel,
        grid=(n // tn, m // tm),
        in_specs=[pl.BlockSpec((tm, k), lambda j, i: (i, 0)),
                  pl.BlockSpec((None, k, tn), lambda j, i: (layer, 0, j)),
                  pl.BlockSpec((None, 1, tn), lambda j, i: (layer, 0, j))],
        out_specs=pl.BlockSpec((tm, tn), lambda j, i: (i, j)),
        out_shape=jax.ShapeDtypeStruct((m, n), out_dtype),
        scratch_shapes=[pltpu.VMEM((k, tn), BF16)],
        compiler_params=_params(("arbitrary", "arbitrary"), 48),
        name="in_proj",
    )(x, w, b.reshape(b.shape[0], 1, n))


def _rope_tables(lay):
    rows = lay.seq // GRID_W
    row = jnp.repeat(jnp.arange(rows, dtype=F32), GRID_W)
    col = jnp.tile(jnp.arange(GRID_W, dtype=F32), rows)
    n_freq = DA_DH // 4
    inv_freq = ROPE_BASE ** (-jnp.arange(n_freq, dtype=F32) / n_freq)
    ang = jnp.concatenate([row[:, None] * inv_freq, col[:, None] * inv_freq], axis=-1)
    cos, sin = jnp.cos(ang), jnp.sin(ang)
    cos_l = jnp.tile(cos, (1, 4))
    sin_l = jnp.tile(jnp.concatenate([-sin, sin], axis=-1), (1, 2))
    cos_l = jnp.concatenate([jnp.ones((lay.ctx, DA_DV), F32), cos_l], axis=0)
    sin_l = jnp.concatenate([jnp.zeros((lay.ctx, DA_DV), F32), sin_l], axis=0)
    return cos_l, sin_l


def _qkv_prep_kernel(zq_ref, zk_ref, cos_ref, sin_ref, q_ref, kt_ref):
    cosf = cos_ref[...]
    sinf = sin_ref[...]
    lane = lax.broadcasted_iota(jnp.int32, cosf.shape, 1)
    first_half = (lane % DA_DH) < (DA_DH // 2)

    def rope(x):
        x = x.astype(F32)
        partner = jnp.where(first_half, pltpu.roll(x, DA_DV - DA_DH // 2, 1), pltpu.roll(x, DA_DH // 2, 1))
        return x * cosf + partner * sinf

    q_scale = (DA_DH ** -0.5) * math.log2(math.e)
    for h in range(DA_HEADS):
        sl = slice(h * DA_DV, (h + 1) * DA_DV)
        q_ref[:, sl] = (rope(zq_ref[:, sl]) * q_scale).astype(BF16)
        kt_ref[sl, :] = rope(zk_ref[:, sl]).T.astype(BF16)


def _qkv_prep(lay, z, cos_l, sin_l):
    tm = lay.tm
    zspec = lambda cb: pl.BlockSpec((tm, DA_W), lambda i: (i, cb))
    tspec = pl.BlockSpec((tm, DA_DV), lambda i: (i % lay.tpb, 0))
    return pl.pallas_call(
        _qkv_prep_kernel,
        grid=(lay.n_tiles,),
        in_specs=[zspec(ZC_Q), zspec(ZC_K), tspec, tspec],
        out_specs=[pl.BlockSpec((tm, DA_W), lambda i: (i, 0)),
                   pl.BlockSpec((None, DA_W, tm), lambda i: (i // lay.tpb, 0, i % lay.tpb))],
        out_shape=[jax.ShapeDtypeStruct((lay.n_all, DA_W), BF16),
                   jax.ShapeDtypeStruct((lay.batch, DA_W, lay.s_all), BF16)],
        compiler_params=_params(("arbitrary",), 32),
        name="qkv_prep",
    )(z, z, cos_l, sin_l)


def _attn_kernel(lam_ref, q_ref, kt_ref, v_ref, g_ref, o_ref, *, ctx_tiles, ctx_len, lam_init):
    lp = lam_ref[...]
    lam = (jnp.exp(jnp.sum(lp[0:1] * lp[1:2], axis=-1, keepdims=True))
           - jnp.exp(jnp.sum(lp[2:3] * lp[3:4], axis=-1, keepdims=True)) + lam_init)

    def attend(n_k):
        for hh in range(ATTN_HEADS_PER_STEP):
            hs = slice(hh * DA_DV, (hh + 1) * DA_DV)
            q = q_ref[:, hs]
            lane = lax.broadcasted_iota(jnp.int32, q.shape, 1)
            zero = jnp.zeros_like(q)
            kt = kt_ref[hs, :n_k]
            v = v_ref[:n_k, hs]

            def softmax_v(qm):
                s = jnp.dot(qm, kt, preferred_element_type=F32)
                p = jnp.exp2(s - jnp.max(s, axis=-1, keepdims=True))
                l = jnp.sum(p, axis=-1, keepdims=True)
                return jnp.dot(p.astype(BF16), v, preferred_element_type=F32) / l

            o = softmax_v(jnp.where(lane < DA_DH, q, zero)) - lam * softmax_v(jnp.where(lane >= DA_DH, q, zero))
            y = o * lax.rsqrt(jnp.mean(o * o, axis=-1, keepdims=True) + LN_EPS)
            o_ref[:, hs] = (y * g_ref[...] * (1.0 - lam_init)).astype(o_ref.dtype)

    is_ctx = pl.program_id(2) < ctx_tiles

    @pl.when(is_ctx)
    def _():
        attend(ctx_len)

    @pl.when(jnp.logical_not(is_ctx))
    def _():
        attend(kt_ref.shape[1])


ATTN_HEADS_PER_STEP = 4


def _attention(lay, q, kt, z, lam_p, norm_g, layer_idx):
    tq = lay.tm
    hw = ATTN_HEADS_PER_STEP * DA_DV
    lam_init = 0.8 - 0.6 * math.exp(-0.3 * layer_idx)
    kern = functools.partial(_attn_kernel, ctx_tiles=lay.ct, ctx_len=lay.ctx, lam_init=lam_init)
    v_cb = ZC_V * (1024 // hw)
    return pl.pallas_call(
        kern,
        grid=(lay.batch, DA_W // hw, lay.tpb),
        in_specs=[pl.BlockSpec((4, DA_DH), lambda b, h, i: (0, 0)),
                  pl.BlockSpec((tq, hw), lambda b, h, i: (b * lay.tpb + i, h)),
                  pl.BlockSpec((None, hw, lay.s_all), lambda b, h, i: (b, h, 0)),
                  pl.BlockSpec((lay.s_all, hw), lambda b, h, i: (b, v_cb + h)),
                  pl.BlockSpec((1, DA_DV), lambda b, h, i: (0, 0))],
        out_specs=pl.BlockSpec((tq, hw), lambda b, h, i: (b * lay.tpb + i, h)),
        out_shape=jax.ShapeDtypeStruct((lay.n_all, DA_W), BF16),
        compiler_params=_params(("arbitrary", "arbitrary", "arbitrary"), 56),
        name="diff_attn",
    )(lam_p, q, kt, z, norm_g.reshape(1, DA_DV))


def _conv_kernel(v_ref, g_ref, vp_ref, gp_ref, vn_ref, gn_ref, dw_ref, db_ref, lg_ref, lb_ref,
                 o_ref, hp_ref, acc_ref, sh_ref, *, tpb, ct):
    tm = v_ref.shape[0]
    ti = pl.program_id(0) % tpb
    has_prev = jnp.logical_and(ti != 0, ti != ct)
    has_next = jnp.logical_and(ti != ct - 1, ti != tpb - 1)
    glu = lambda v, g: v.astype(F32) * jax.nn.sigmoid(g.astype(F32))
    hp_ref[0:CONV_HALO, :] = jnp.where(has_prev, glu(vp_ref[...], gp_ref[...]), 0.0)
    hp_ref[CONV_HALO:CONV_HALO + tm, :] = glu(v_ref[...], g_ref[...])
    hp_ref[CONV_HALO + tm:, :] = jnp.where(has_next, glu(vn_ref[...], gn_ref[...]), 0.0)

    first = CONV_HALO - CONV_K // 2
    rows = tm // 2
    span = sh_ref.shape[1]
    n_copy = 0
    for c in range(CONV_W // V7X_LANES):
        cs = pl.ds(c * V7X_LANES, V7X_LANES)
        for r0 in range(0, tm, rows):
            acc = jnp.zeros((rows, V7X_LANES), F32)
            for res in range(V7X_SUBLANES):
                buf = sh_ref.at[n_copy % 2]
                n_copy += 1
                buf[...] = hp_ref[pl.ds(r0 + res, span), cs]
                win = buf[...]
                for off in range(res, first + CONV_K, V7X_SUBLANES):
                    k = off - first
                    if k >= 0:
                        acc = acc + dw_ref[k:k + 1, cs] * win[off - res:off - res + rows]
            acc_ref[r0:r0 + rows, cs] = acc
    y = _ln(acc_ref[...] + db_ref[...]) * lg_ref[...] + lb_ref[...]
    o_ref[...] = (y * jax.nn.sigmoid(y)).astype(o_ref.dtype)


def _conv_branch(lay, z, dw, db, ln_g, ln_b):
    tm = lay.tm
    hb = tm // CONV_HALO
    n_halo = lay.n_all // CONV_HALO
    main = lambda cb: pl.BlockSpec((tm, CONV_W), lambda i: (i, cb))
    prev = lambda cb: pl.BlockSpec((CONV_HALO, CONV_W), lambda i: (jnp.maximum(i * hb - 1, 0), cb))
    nxt = lambda cb: pl.BlockSpec((CONV_HALO, CONV_W), lambda i: (jnp.minimum((i + 1) * hb, n_halo - 1), cb))
    vec = pl.BlockSpec((1, CONV_W), lambda i: (0, 0))
    kern = functools.partial(_conv_kernel, tpb=lay.tpb, ct=lay.ct)
    return pl.pallas_call(
        kern,
        grid=(lay.n_tiles,),
        in_specs=[main(ZC_CONV_V), main(ZC_CONV_G), prev(ZC_CONV_V), prev(ZC_CONV_G),
                  nxt(ZC_CONV_V), nxt(ZC_CONV_G),
                  pl.BlockSpec((CONV_K, CONV_W), lambda i: (0, 0)), vec, vec, vec],
        out_specs=pl.BlockSpec((tm, CONV_W), lambda i: (i, 0)),
        out_shape=jax.ShapeDtypeStruct((lay.n_all, CONV_W), BF16),
        scratch_shapes=[pltpu.VMEM((tm + 2 * CONV_HALO, CONV_W), F32), pltpu.VMEM((tm, CONV_W), F32),
                        pltpu.VMEM((2, tm // 2 + (CONV_HALO + CONV_K // 2) // V7X_SUBLANES * V7X_SUBLANES,
                                    V7X_LANES), F32)],
        compiler_params=_params(("arbitrary",), 32),
        name="conv_branch",
    )(z, z, z, z, z, z, dw, db.reshape(1, -1), ln_g.reshape(1, -1), ln_b.reshape(1, -1))


LRU_CH = V7X_LANES
LRU_PAD = V7X_SUBLANES


def _gelu_tanh(x):
    return 0.5 * x * (1.0 + jnp.tanh(math.sqrt(2.0 / math.pi) * (x + 0.044715 * (x * x * x))))


def _softplus(x):
    return jnp.maximum(x, 0.0) + jnp.log1p(jnp.exp(-jnp.abs(x)))


def _tile_scan(a, b, reverse):
    row = lax.broadcasted_iota(jnp.int32, a.shape, 1)
    for s in (1, 2, 4):
        if reverse:
            valid = row < V7X_SUBLANES - s
            shift = V7X_SUBLANES - s
        else:
            valid = row >= s
            shift = s
        a_sh = pltpu.roll(a, shift, 1)
        b_sh = pltpu.roll(b, shift, 1)
        b = jnp.where(valid, a * b_sh + b, b)
        a = jnp.where(valid, a * a_sh, a)
    return a, b


def _lru_kernel(zx_ref, zg_ref, cw_ref, cb_ref, gw_ref, gb_ref, lam_ref, o_ref,
                xp_ref, af_ref, bf_ref, ab_ref, bb_ref, *, ctx_len, chunk):
    s_all = zx_ref.shape[0]
    ch = zx_ref.shape[1]
    xp_ref[0:LRU_PAD, :] = jnp.zeros((LRU_PAD, ch), F32)
    xp_ref[LRU_PAD:LRU_PAD + s_all, :] = zx_ref[...].astype(F32)
    xp_ref[LRU_PAD + s_all:, :] = jnp.zeros((LRU_PAD, ch), F32)

    sp = _softplus(-lam_ref[...])
    gw = gw_ref[...]
    seg_starts = (0, ctx_len)
    seg_ends = (ctx_len, s_all)
    for r0 in range(0, s_all, chunk):
        row = r0 + lax.broadcasted_iota(jnp.int32, (chunk, ch), 0)
        xs = jnp.zeros((chunk, ch), F32) + cb_ref[...]
        for k in range(LRU_CONV_K):
            off = k - 2
            x = xp_ref[pl.ds(LRU_PAD + r0 + off, chunk), :]
            if off < 0 and r0 in seg_starts:
                x = jnp.where(row - r0 < -off, 0.0, x)
            if off > 0 and r0 + chunk in seg_ends:
                x = jnp.where(row - r0 >= chunk - off, 0.0, x)
            xs = xs + cw_ref[k:k + 1, :] * x
        pre = jnp.dot(xs.astype(BF16), gw, preferred_element_type=F32) + gb_ref[...]
        for d, (a_ref, b_ref) in enumerate(((af_ref, bf_ref), (ab_ref, bb_ref))):
            r = jax.nn.sigmoid(pre[:, (2 * d) * ch:(2 * d + 1) * ch])
            i = jax.nn.sigmoid(pre[:, (2 * d + 1) * ch:(2 * d + 2) * ch])
            a = jnp.exp(-LRU_C * r * sp[d:d + 1, :])
            b = jnp.sqrt(1.0 - a * a) * (i * xs)
            a_t, b_t = _tile_scan(a.reshape(chunk // V7X_SUBLANES, V7X_SUBLANES, ch),
                                  b.reshape(chunk // V7X_SUBLANES, V7X_SUBLANES, ch), reverse=(d == 1))
            a_ref[pl.ds(r0, chunk), :] = a_t.reshape(chunk, ch)
            b_ref[pl.ds(r0, chunk), :] = b_t.reshape(chunk, ch)

    n_t = s_all // V7X_SUBLANES
    c_t = ctx_len // V7X_SUBLANES

    def fwd_step(t, carry):
        rows = pl.ds(pl.multiple_of(t * V7X_SUBLANES, V7X_SUBLANES), V7X_SUBLANES)
        h = bf_ref[rows, :] + af_ref[rows, :] * carry
        bf_ref[rows, :] = h
        return jnp.broadcast_to(h[V7X_SUBLANES - 1:, :], h.shape)

    def bwd_step(t, carry, hi):
        t = hi - 1 - t
        rows = pl.ds(pl.multiple_of(t * V7X_SUBLANES, V7X_SUBLANES), V7X_SUBLANES)
        h = bb_ref[rows, :] + ab_ref[rows, :] * carry
        bb_ref[rows, :] = h
        return jnp.broadcast_to(h[0:1, :], h.shape)

    zero = jnp.zeros((V7X_SUBLANES, ch), F32)
    lax.fori_loop(0, n_t, fwd_step, zero)
    carry = lax.fori_loop(0, c_t, functools.partial(bwd_step, hi=c_t), zero)
    lax.fori_loop(0, n_t - c_t, functools.partial(bwd_step, hi=n_t), carry)

    for r0 in range(0, s_all, chunk):
        rows = pl.ds(r0, chunk)
        hsum = bf_ref[rows, :] + bb_ref[rows, :]
        o_ref[rows, :] = (hsum * _gelu_tanh(zg_ref[rows, :].astype(F32))).astype(o_ref.dtype)


def _lru_gate_matrix(gate_w):
    n_grp = LRU_W // LRU_CH
    per = LRU_CH // LRU_BS
    w = gate_w.reshape(4, n_grp, per, LRU_BS, LRU_BS)
    eye = jnp.eye(per, dtype=gate_w.dtype)
    full = jnp.einsum("gcpjk,pq->cpjgqk", w, eye)
    return full.reshape(n_grp, LRU_CH, 4 * LRU_CH)


def _lru_branch(lay, z, conv_w, conv_b, gate_w, gate_b, lam):
    s_all = lay.s_all
    n_grp = LRU_W // LRU_CH
    chunk = lay.tm
    gw = _lru_gate_matrix(gate_w).astype(BF16)
    gb = gate_b.reshape(4, n_grp, LRU_CH).transpose(1, 0, 2).reshape(n_grp, 1, 4 * LRU_CH)
    kern = functools.partial(_lru_kernel, ctx_len=lay.ctx, chunk=chunk)
    cb_x = ZC_LRU_X * (1024 // LRU_CH)
    cb_g = ZC_LRU_G * (1024 // LRU_CH)
    return pl.pallas_call(
        kern,
        grid=(lay.batch, n_grp),
        in_specs=[pl.BlockSpec((s_all, LRU_CH), lambda b, c: (b, cb_x + c)),
                  pl.BlockSpec((s_all, LRU_CH), lambda b, c: (b, cb_g + c)),
                  pl.BlockSpec((LRU_CONV_K, LRU_CH), lambda b, c: (0, c)),
                  pl.BlockSpec((1, LRU_CH), lambda b, c: (0, c)),
                  pl.BlockSpec((None, LRU_CH, 4 * LRU_CH), lambda b, c: (c, 0, 0)),
                  pl.BlockSpec((None, 1, 4 * LRU_CH), lambda b, c: (c, 0, 0)),
                  pl.BlockSpec((2, LRU_CH), lambda b, c: (0, c))],
        out_specs=pl.BlockSpec((s_all, LRU_CH), lambda b, c: (b, c)),
        out_shape=jax.ShapeDtypeStruct((lay.n_all, LRU_W), BF16),
        scratch_shapes=[pltpu.VMEM((s_all + 2 * LRU_PAD, LRU_CH), F32)]
        + [pltpu.VMEM((s_all, LRU_CH), F32) for _ in range(4)],
        compiler_params=_params(("arbitrary", "arbitrary"), 48),
        name="rglru",
    )(z, z, conv_w, conv_b.reshape(1, -1), gw, gb, lam)


def _merge_kernel(hc_ref, ha_ref, hl_ref, g0a, g0b, g1a, g1b, g2a, g2b, wc_ref, wa_ref, wl_ref, o_ref):
    half = D_MODEL // 2
    branches = ((hc_ref, wc_ref, (g0a, g0b)), (ha_ref, wa_ref, (g1a, g1b)), (hl_ref, wl_ref, (g2a, g2b)))
    for half_idx in range(2):
        cs = slice(half_idx * half, (half_idx + 1) * half)
        acc = None
        for h_ref, w_ref, gates in branches:
            p = (jax.nn.sigmoid(gates[half_idx][...].astype(F32))
                 * jnp.dot(h_ref[...], w_ref[:, cs], preferred_element_type=F32))
            acc = p if acc is None else acc + p
        o_ref[:, cs] = acc.astype(o_ref.dtype)


def _merge(lay, z, hc, ha, hl, wc, wa, wl, layer):
    tm = lay.tm
    hspec = pl.BlockSpec((tm, 1024), lambda i: (i, 0))
    gspec = lambda cb: pl.BlockSpec((tm, 1024), lambda i: (i, ZC_MERGE + cb))
    wspec = pl.BlockSpec((None, 1024, D_MODEL), lambda i: (layer, 0, 0))
    return pl.pallas_call(
        _merge_kernel,
        grid=(lay.n_tiles,),
        in_specs=[hspec, hspec, hspec] + [gspec(cb) for cb in range(6)] + [wspec, wspec, wspec],
        out_specs=pl.BlockSpec((tm, D_MODEL), lambda i: (i, 0)),
        out_shape=jax.ShapeDtypeStruct((lay.n_all, D_MODEL), BF16),
        compiler_params=_params(("arbitrary",), 56),
        name="merge",
    )(hc, ha, hl, z, z, z, z, z, z, wc, wa, wl)


def _mix_kernel(m_ref, h_ref, wo_ref, g1_ref, lg_ref, lb_ref, sh_ref, sc_ref, rw_ref, rb_ref,
                h_out, u_out, idx_out, wgt_out, rank_out, cnt_out, run_ref):
    @pl.when(pl.program_id(0) == 0)
    def _():
        run_ref[...] = jnp.zeros_like(run_ref)

    mix = jnp.dot(m_ref[...], wo_ref[...], preferred_element_type=F32)
    x = _ln(DEEPNORM_ALPHA * h_ref[...] + g1_ref[...] * mix) * lg_ref[...] + lb_ref[...]
    h_out[...] = x
    u = _ln(x) * (1.0 + sc_ref[...]) + sh_ref[...]
    u_hi = u.astype(BF16)
    u_out[...] = u_hi

    u_lo = (u - u_hi.astype(F32)).astype(BF16)
    hi_all = jnp.dot(u_hi, rw_ref[...], preferred_element_type=F32)
    logits = (hi_all[:, :V7X_LANES] + hi_all[:, V7X_LANES:]
              + jnp.dot(u_lo, rw_ref[:, :V7X_LANES], preferred_element_type=F32) + rb_ref[...])
    tm = logits.shape[0]
    out_lane = lax.broadcasted_iota(jnp.int32, logits.shape, 1)
    lane = out_lane.astype(F32)
    idx_acc = jnp.zeros(logits.shape, F32)
    val_acc = jnp.zeros(logits.shape, F32)
    chosen = jnp.zeros(logits.shape, F32)
    hits = []
    top = None
    denom = None
    for k in range(TOP_K):
        mx = jnp.max(logits, axis=-1, keepdims=True)
        idx = jnp.min(jnp.where(logits == mx, lane, float(V7X_LANES)), axis=-1, keepdims=True)
        hit = lane == idx
        hits.append(hit)
        chosen = jnp.where(hit, 1.0, chosen)
        logits = jnp.where(hit, -jnp.inf, logits)
        if k == 0:
            top = mx
        e = jnp.exp(mx - top)
        denom = e if k == 0 else denom + e
        idx_acc = jnp.where(out_lane == k, idx, idx_acc)
        val_acc = jnp.where(out_lane == k, e, val_acc)
    idx_out[...] = idx_acc.astype(jnp.int32)
    wgt_out[...] = val_acc / denom

    earlier = (lax.broadcasted_iota(jnp.int32, (tm, tm), 0) > lax.broadcasted_iota(jnp.int32, (tm, tm), 1))
    before = jnp.dot(jnp.where(earlier, 1.0, 0.0).astype(BF16), chosen.astype(BF16),
                     preferred_element_type=F32) + run_ref[...]
    rank_acc = jnp.zeros(logits.shape, F32)
    for k in range(TOP_K):
        r = jnp.sum(jnp.where(hits[k], before, 0.0), axis=-1, keepdims=True)
        rank_acc = jnp.where(out_lane == k, r, rank_acc)
    rank_out[...] = rank_acc.astype(jnp.int32)
    run_ref[...] = run_ref[...] + jnp.sum(chosen, axis=0, keepdims=True)
    cnt_out[...] = jnp.broadcast_to(run_ref[...], cnt_out.shape)


def _mix(lay, m, h, wo, layer, g1, ln_g, ln_b, sh2, sc2, router_w, router_b):
    tm = lay.tm
    d = D_MODEL
    row = pl.BlockSpec((tm, d), lambda i: (i, 0))
    mod = pl.BlockSpec((None, 1, d), lambda i: (lay.mod_row(i), 0, 0))
    vec = pl.BlockSpec((1, d), lambda i: (0, 0))
    small = pl.BlockSpec((tm, V7X_LANES), lambda i: (i, 0))
    pad = V7X_LANES - N_EXPERTS
    rw = jnp.pad(router_w, ((0, 0), (0, pad)))
    rw_hi = rw.astype(BF16)
    rw = jnp.concatenate([rw_hi, (rw - rw_hi.astype(F32)).astype(BF16)], axis=1)
    rb = jnp.pad(router_b, (0, pad), constant_values=-jnp.inf).reshape(1, V7X_LANES)
    return pl.pallas_call(
        _mix_kernel,
        grid=(lay.n_tiles,),
        in_specs=[row, row, pl.BlockSpec((None, d, d), lambda i: (layer, 0, 0)), mod, vec, vec, mod, mod,
                  pl.BlockSpec((d, 2 * V7X_LANES), lambda i: (0, 0)),
                  pl.BlockSpec((1, V7X_LANES), lambda i: (0, 0))],
        out_specs=[row, row, small, small, small, pl.BlockSpec((V7X_SUBLANES, V7X_LANES), lambda i: (0, 0))],
        out_shape=[jax.ShapeDtypeStruct((lay.n_all, d), F32),
                   jax.ShapeDtypeStruct((lay.n_all, d), BF16),
                   jax.ShapeDtypeStruct((lay.n_all, V7X_LANES), jnp.int32),
                   jax.ShapeDtypeStruct((lay.n_all, V7X_LANES), F32),
                   jax.ShapeDtypeStruct((lay.n_all, V7X_LANES), jnp.int32),
                   jax.ShapeDtypeStruct((V7X_SUBLANES, V7X_LANES), F32)],
        scratch_shapes=[pltpu.VMEM((1, V7X_LANES), F32)],
        compiler_params=_params(("arbitrary",), 56),
        name="mix_ln1_router",
    )(m, h, wo, g1, ln_g.reshape(1, d), ln_b.reshape(1, d), sh2, sc2, rw, rb)


W1_PERM = 256


def _deinterleave_matrix():
    half = W1_PERM // 2
    src = jnp.arange(W1_PERM, dtype=jnp.int32)[:, None]
    dst = jnp.arange(W1_PERM, dtype=jnp.int32)[None, :]
    want = jnp.where(dst < half, 2 * dst, 2 * (dst - half) + 1)
    return (src == want).astype(BF16)


def _moe_kernel(blk_e_ref, first_ref, next_ref, n_used_ref,
                x_ref, p_ref, b1g_ref, b1l_ref, b2_ref, w1_hbm, w2_hbm,
                o_ref, st1_ref, st2_ref, w1g_ref, w1l_ref, w2bf_ref, sem, *, e0):
    i = pl.program_id(0)
    used = i < n_used_ref[0]

    def weight_copies(e):
        return (pltpu.make_async_copy(w1_hbm.at[e], st1_ref, sem.at[0]),
                pltpu.make_async_copy(w2_hbm.at[e], st2_ref, sem.at[1]))

    @pl.when(jnp.logical_and(used, first_ref[i] == 1))
    def _():
        @pl.when(i == 0)
        def _():
            for cp in weight_copies(e0 + blk_e_ref[0]):
                cp.start()

        for cp in weight_copies(e0 + blk_e_ref[i]):
            cp.wait()

        half = W1_PERM // 2

        def slab(r, carry):
            rows = pl.ds(pl.multiple_of(r * W1_PERM, W1_PERM), W1_PERM)
            for j in range(st1_ref.shape[1] // W1_PERM):
                w = st1_ref[rows, j * W1_PERM:(j + 1) * W1_PERM].astype(BF16)
                y = jnp.dot(w, p_ref[...], preferred_element_type=F32)
                w1g_ref[rows, j * half:(j + 1) * half] = y[:, :half].astype(BF16)
                w1l_ref[rows, j * half:(j + 1) * half] = y[:, half:].astype(BF16)
            return carry

        lax.fori_loop(0, st1_ref.shape[0] // W1_PERM, slab, 0)
        w2bf_ref[...] = st2_ref[...].astype(BF16)

        @pl.when(next_ref[i] >= 0)
        def _():
            for cp in weight_copies(e0 + next_ref[i]):
                cp.start()

    @pl.when(used)
    def _():
        x = x_ref[...]
        gl = jnp.minimum(jnp.dot(x, w1g_ref[...], preferred_element_type=F32) + b1g_ref[...], SWIGLU_LIMIT)
        lin = jnp.clip(jnp.dot(x, w1l_ref[...], preferred_element_type=F32) + b1l_ref[...],
                       -SWIGLU_LIMIT, SWIGLU_LIMIT)
        act = gl * jax.nn.sigmoid(SWIGLU_ALPHA * gl) * (lin + 1.0)
        y = jnp.dot(act.astype(BF16), w2bf_ref[...], preferred_element_type=F32) + b2_ref[...]
        o_ref[...] = y.astype(o_ref.dtype)

    @pl.when(jnp.logical_not(used))
    def _():
        o_ref[...] = jnp.zeros_like(o_ref)


def _moe_ffn_blocks(xs, blk_e, blk_first, blk_next, n_used, w1, w2, b1g, b1l, b2, layer):
    n_slots, d = xs.shape
    n_blocks = n_slots // MOE_ROWS
    f = w2.shape[1]
    e0 = layer * N_EXPERTS
    bsel = lambda width: pl.BlockSpec((None, 1, width), lambda i, be, bf, bn, nu: (e0 + be[i], 0, 0))
    grid_spec = pltpu.PrefetchScalarGridSpec(
        num_scalar_prefetch=4,
        grid=(n_blocks,),
        in_specs=[pl.BlockSpec((MOE_ROWS, d), lambda i, be, bf, bn, nu: (i, 0)),
                  pl.BlockSpec((W1_PERM, W1_PERM), lambda i, be, bf, bn, nu: (0, 0)),
                  bsel(f), bsel(f), bsel(d),
                  pl.BlockSpec(memory_space=pl.ANY), pl.BlockSpec(memory_space=pl.ANY)],
        out_specs=pl.BlockSpec((MOE_ROWS, d), lambda i, be, bf, bn, nu: (i, 0)),
        scratch_shapes=[pltpu.VMEM((d, 2 * f), F32), pltpu.VMEM((f, d), F32),
                        pltpu.VMEM((d, f), BF16), pltpu.VMEM((d, f), BF16), pltpu.VMEM((f, d), BF16),
                        pltpu.SemaphoreType.DMA((2,))],
    )
    return pl.pallas_call(
        functools.partial(_moe_kernel, e0=e0),
        grid_spec=grid_spec,
        out_shape=jax.ShapeDtypeStruct((n_slots, d), BF16),
        compiler_params=_params(("arbitrary",), 58),
        name="moe_ffn",
    )(blk_e, blk_first, blk_next, n_used, xs, _deinterleave_matrix(), b1g, b1l, b2, w1, w2)


def _moe_dispatch(top_idx, rank, counts, n_tok):
    n_assign = n_tok * TOP_K
    n_blocks = -(-n_assign // MOE_ROWS) + N_EXPERTS
    n_slots = n_blocks * MOE_ROWS
    padded = (counts + MOE_ROWS - 1) // MOE_ROWS * MOE_ROWS
    pad_end = jnp.cumsum(padded)
    pad_start = pad_end - padded
    slot_of = pad_start.at[top_idx].get(mode="promise_in_bounds") + rank
    tok = jnp.broadcast_to(jnp.arange(n_tok, dtype=jnp.int32)[:, None], (n_tok, TOP_K))
    slot_tok = (jnp.arange(n_slots, dtype=jnp.int32) % n_tok).at[slot_of.reshape(-1)].set(
        tok.reshape(-1), mode="promise_in_bounds", unique_indices=True)
    blk_idx = jnp.arange(n_blocks, dtype=jnp.int32)
    blk_e = jnp.minimum(jnp.sum((blk_idx * MOE_ROWS)[:, None] >= pad_end[None, :], axis=1),
                        N_EXPERTS - 1).astype(jnp.int32)
    blk_first = jnp.concatenate([jnp.ones((1,), jnp.int32), (blk_e[1:] != blk_e[:-1]).astype(jnp.int32)])
    n_used = (pad_end[-1] // MOE_ROWS).astype(jnp.int32)
    starts = jnp.where(jnp.logical_and(blk_first == 1, blk_idx < n_used), blk_idx, n_blocks)
    nxt = jnp.concatenate([lax.cummin(starts, axis=0, reverse=True)[1:], jnp.full((1,), n_blocks, jnp.int32)])
    blk_next = jnp.where(nxt < n_blocks, blk_e.at[jnp.minimum(nxt, n_blocks - 1)].get(mode="promise_in_bounds"),
                         -1).astype(jnp.int32)
    return slot_tok, slot_of, blk_e, blk_first, blk_next, n_used.reshape(1)


def _final_kernel(h_ref, y_ref, w_ref, g2_ref, lg_ref, lb_ref, *rest, with_next):
    if with_next:
        sh_ref, sc_ref, h_out, u_out = rest
    else:
        (h_out,) = rest
    w = w_ref[...]
    y = w[:, 0:1] * y_ref[0].astype(F32)
    for k in range(1, TOP_K):
        y = y + w[:, k:k + 1] * y_ref[k].astype(F32)
    x = _ln(DEEPNORM_ALPHA * h_ref[...] + g2_ref[...] * y) * lg_ref[...] + lb_ref[...]
    h_out[...] = x
    if with_next:
        u_out[...] = (_ln(x) * (1.0 + sc_ref[...]) + sh_ref[...]).astype(u_out.dtype)


def _final(lay, h, yk, top_w, g2, ln_g, ln_b, next_mod):
    tm = lay.tm
    d = D_MODEL
    row = pl.BlockSpec((tm, d), lambda i: (i, 0))
    mod = pl.BlockSpec((None, 1, d), lambda i: (lay.mod_row(i), 0, 0))
    vec = pl.BlockSpec((1, d), lambda i: (0, 0))
    in_specs = [row, pl.BlockSpec((TOP_K, tm, d), lambda i: (0, i, 0)),
                pl.BlockSpec((tm, V7X_LANES), lambda i: (i, 0)), mod, vec, vec]
    args = [h, yk, top_w, g2, ln_g.reshape(1, d), ln_b.reshape(1, d)]
    out_specs = [row]
    out_shape = [jax.ShapeDtypeStruct((lay.n_all, d), F32)]
    if next_mod is not None:
        in_specs += [mod, mod]
        args += list(next_mod)
        out_specs.append(row)
        out_shape.append(jax.ShapeDtypeStruct((lay.n_all, d), BF16))
    outs = pl.pallas_call(
        functools.partial(_final_kernel, with_next=next_mod is not None),
        grid=(lay.n_tiles,),
        in_specs=in_specs,
        out_specs=out_specs,
        out_shape=out_shape,
        compiler_params=_params(("arbitrary",), 48),
        name="combine_ln2",
    )(*args)
    return outs if next_mod is not None else (outs[0], None)


def kernel(x, c, ctx, c_ctx, w_ada, b_ada, w_in, b_in, conv_dw, conv_db, conv_ln_g, conv_ln_b, w_conv_out,
           da_lambda, da_norm_g, w_da_out, lru_conv_w, lru_conv_b, lru_gate_w, lru_gate_b, lru_lambda,
           w_lru_out, w_out, ln1_g, ln1_b, router_w, router_b, moe_w1, moe_b1, moe_w2, moe_b2, ln2_g, ln2_b):
    batch, seq, d = x.shape
    lay = _Layout(batch, seq, ctx.shape[1])
    h = jnp.concatenate([ctx, x], axis=1).reshape(lay.n_all, d)

    c_rows = jnp.zeros((16, d), F32).at[:batch].set(c).at[batch].set(c_ctx)
    mods = []
    for l in range(DEPTH):
        ada = _ada(c_rows, w_ada, b_ada, l)[:batch + 1]
        mods.append([m.reshape(batch + 1, 1, d) for m in jnp.split(ada, 6, axis=-1)])

    n_exp, d_in, f2 = moe_w1.shape[1:]
    w1_all = moe_w1.reshape(DEPTH * n_exp, d_in, f2)
    w2_all = moe_w2.reshape(DEPTH * n_exp, f2 // 2, d)
    b1_all = moe_b1.reshape(DEPTH * n_exp, 1, f2)
    b1g_all, b1l_all = b1_all[:, :, 0::2], b1_all[:, :, 1::2]
    b2_all = moe_b2.reshape(DEPTH * n_exp, 1, d)
    wc_bf, wa_bf, wl_bf, wo_bf = (w.astype(BF16) for w in (w_conv_out, w_da_out, w_lru_out, w_out))

    cos_l, sin_l = _rope_tables(lay)
    u = _ln_mod(lay, h, mods[0][0], mods[0][1])
    for l in range(DEPTH):
        sh1, sc1, g1, sh2, sc2, g2 = mods[l]
        tm_in = next((t for t in (1024, 512) if lay.n_all % t == 0), lay.tm)
        z = _mm(u, w_in, b_in, l, tm_in, 1024, BF16)

        hc = _conv_branch(lay, z, conv_dw[l], conv_db[l], conv_ln_g[l], conv_ln_b[l])
        q, kt = _qkv_prep(lay, z, cos_l, sin_l)
        ha = _attention(lay, q, kt, z, da_lambda[l], da_norm_g[l], l)
        hl = _lru_branch(lay, z, lru_conv_w[l], lru_conv_b[l], lru_gate_w[l], lru_gate_b[l], lru_lambda[l])
        m = _merge(lay, z, hc, ha, hl, wc_bf, wa_bf, wl_bf, l)
        h, u2, top_idx, top_w, rank, counts = _mix(lay, m, h, wo_bf, l, g1, ln1_g[l], ln1_b[l],
                                                   sh2, sc2, router_w[l], router_b[l])

        slot_tok, slot_of, blk_e, blk_first, blk_next, n_used = _moe_dispatch(
            top_idx[:, :TOP_K], rank[:, :TOP_K], counts[0, :N_EXPERTS].astype(jnp.int32), lay.n_all)
        xs = u2.at[slot_tok].get(mode="promise_in_bounds")
        outs = _moe_ffn_blocks(xs, blk_e, blk_first, blk_next, n_used, w1_all, w2_all,
                               b1g_all, b1l_all, b2_all, l)
        yk = outs.at[slot_of.T].get(mode="promise_in_bounds")
        next_mod = (mods[l + 1][0], mods[l + 1][1]) if l + 1 < DEPTH else None
        h, u = _final(lay, h, yk, top_w, g2, ln2_g[l], ln2_b[l], next_mod)

    return h.reshape(batch, lay.s_all, d)[:, lay.ctx:, :]
```

```python
import functools
import math

import jax
import jax.numpy as jnp
from jax import lax
from jax.experimental import pallas as pl
from jax.experimental.pallas import tpu as pltpu

D_MODEL = 2048
DEPTH = 2
GRID_W = 64
CONV_W = 1024
CONV_K = 31
DA_HEADS = 8
DA_DH = 64
DA_DV = 2 * DA_DH
DA_W = DA_HEADS * DA_DV
ROPE_BASE = 10000.0
LRU_W = 1024
LRU_BLOCKS = 16
LRU_BS = LRU_W // LRU_BLOCKS
LRU_CONV_K = 4
LRU_C = 8.0
N_BRANCH = 3
N_EXPERTS = 32
TOP_K = 4
D_EXPERT = 1024
SWIGLU_LIMIT = 7.0
SWIGLU_ALPHA = 1.702
LN_EPS = 1e-5
DEEPNORM_ALPHA = (2 * DEPTH) ** 0.25
IN_SPLIT = (2 * CONV_W, DA_W, DA_W, DA_W, LRU_W, LRU_W, N_BRANCH * D_MODEL)
IN_W = sum(IN_SPLIT)

V7X_LANES = 128
V7X_SUBLANES = 8
V7X_VMEM_BYTES = 64 * 1024 * 1024

ZC_CONV_V, ZC_CONV_G, ZC_Q, ZC_K, ZC_V, ZC_LRU_X, ZC_LRU_G, ZC_MERGE = 0, 1, 2, 3, 4, 5, 6, 7
CONV_HALO = 16
MOE_ROWS = 256

BF16 = jnp.bfloat16
F32 = jnp.float32


def _params(sem, vmem_mb):
    return pltpu.CompilerParams(dimension_semantics=sem, vmem_limit_bytes=vmem_mb * 1024 * 1024)


def _ln(x):
    mu = jnp.mean(x, axis=-1, keepdims=True)
    xc = x - mu
    var = jnp.mean(xc * xc, axis=-1, keepdims=True)
    return xc * lax.rsqrt(var + LN_EPS)


class _Layout:
    def __init__(self, batch, seq, ctx_len):
        self.batch, self.seq, self.ctx = batch, seq, ctx_len
        self.s_all = ctx_len + seq
        self.n_all = batch * self.s_all
        self.tm = 256 if (ctx_len % 256 == 0 and seq % 256 == 0) else 128
        assert ctx_len % self.tm == 0 and seq % self.tm == 0
        self.tpb = self.s_all // self.tm
        self.ct = ctx_len // self.tm
        self.n_tiles = self.n_all // self.tm

    def mod_row(self, i):
        return jnp.where(i % self.tpb < self.ct, self.batch, i // self.tpb)


def _ada_kernel(c_ref, w_ref, b_ref, o_ref):
    c = c_ref[...]
    a = (c * jax.nn.sigmoid(c)).astype(BF16)
    o_ref[...] = jnp.dot(a, w_ref[...].astype(BF16), preferred_element_type=F32) + b_ref[...]


def _ada(c_rows, w, b, layer):
    m, d = c_rows.shape
    n = w.shape[2]
    tn = 1024
    return pl.pallas_call(
        _ada_kernel,
        grid=(n // tn,),
        in_specs=[pl.BlockSpec((m, d), lambda j: (0, 0)),
                  pl.BlockSpec((None, d, tn), lambda j: (layer, 0, j)),
                  pl.BlockSpec((None, 1, tn), lambda j: (layer, 0, j))],
        out_specs=pl.BlockSpec((m, tn), lambda j: (0, j)),
        out_shape=jax.ShapeDtypeStruct((m, n), F32),
        compiler_params=_params(("arbitrary",), 40),
        name="ada",
    )(c_rows, w, b.reshape(b.shape[0], 1, n))


def _ln_mod_kernel(x_ref, sh_ref, sc_ref, o_ref):
    o_ref[...] = (_ln(x_ref[...]) * (1.0 + sc_ref[...]) + sh_ref[...]).astype(o_ref.dtype)


def _ln_mod(lay, h, shift, scale):
    d = h.shape[1]
    mod_spec = pl.BlockSpec((None, 1, d), lambda i: (lay.mod_row(i), 0, 0))
    return pl.pallas_call(
        _ln_mod_kernel,
        grid=(lay.n_tiles,),
        in_specs=[pl.BlockSpec((lay.tm, d), lambda i: (i, 0)), mod_spec, mod_spec],
        out_specs=pl.BlockSpec((lay.tm, d), lambda i: (i, 0)),
        out_shape=jax.ShapeDtypeStruct(h.shape, BF16),
        compiler_params=_params(("arbitrary",), 32),
        name="ln_mod",
    )(h, shift, scale)


def _mm_kernel(x_ref, w_ref, b_ref, o_ref, wbf_ref):
    @pl.when(pl.program_id(1) == 0)
    def _():
        wbf_ref[...] = w_ref[...].astype(BF16)

    acc = jnp.dot(x_ref[...], wbf_ref[...], preferred_element_type=F32)
    o_ref[...] = (acc + b_ref[...]).astype(o_ref.dtype)


def _mm(x, w, b, layer, tm, tn, out_dtype):
    m, k = x.shape
    n = w.shape[2]
    return pl.pallas_call(
        _mm_kernel,
        grid=(n // tn, m // tm),
        in_specs=[pl.BlockSpec((tm, k), lambda j, i: (i, 0)),
                  pl.BlockSpec((None, k, tn), lambda j, i: (layer, 0, j)),
                  pl.BlockSpec((None, 1, tn), lambda j, i: (layer, 0, j))],
        out_specs=pl.BlockSpec((tm, tn), lambda j, i: (i, j)),
        out_shape=jax.ShapeDtypeStruct((m, n), out_dtype),
        scratch_shapes=[pltpu.VMEM((k, tn), BF16)],
        compiler_params=_params(("arbitrary", "arbitrary"), 48),
        name="in_proj",
    )(x, w, b.reshape(b.shape[0], 1, n))


def _rope_tables(lay):
    rows = lay.seq // GRID_W
    row = jnp.repeat(jnp.arange(rows, dtype=F32), GRID_W)
    col = jnp.tile(jnp.arange(GRID_W, dtype=F32), rows)
    n_freq = DA_DH // 4
    inv_freq = ROPE_BASE ** (-jnp.arange(n_freq, dtype=F32) / n_freq)
    ang = jnp.concatenate([row[:, None] * inv_freq, col[:, None] * inv_freq], axis=-1)
    cos, sin = jnp.cos(ang), jnp.sin(ang)
    cos_l = jnp.tile(cos, (1, 4))
    sin_l = jnp.tile(jnp.concatenate([-sin, sin], axis=-1), (1, 2))
    cos_l = jnp.concatenate([jnp.ones((lay.ctx, DA_DV), F32), cos_l], axis=0)
    sin_l = jnp.concatenate([jnp.zeros((lay.ctx, DA_DV), F32), sin_l], axis=0)
    return cos_l, sin_l


def _qkv_prep_kernel(zq_ref, zk_ref, cos_ref, sin_ref, q_ref, kt_ref):
    cosf = cos_ref[...]
    sinf = sin_ref[...]
    lane = lax.broadcasted_iota(jnp.int32, cosf.shape, 1)
    first_half = (lane % DA_DH) < (DA_DH // 2)

    def rope(x):
        x = x.astype(F32)
        partner = jnp.where(first_half, pltpu.roll(x, DA_DV - DA_DH // 2, 1), pltpu.roll(x, DA_DH // 2, 1))
        return x * cosf + partner * sinf

    q_scale = (DA_DH ** -0.5) * math.log2(math.e)
    for h in range(DA_HEADS):
        sl = slice(h * DA_DV, (h + 1) * DA_DV)
        q_ref[:, sl] = (rope(zq_ref[:, sl]) * q_scale).astype(BF16)
        kt_ref[sl, :] = rope(zk_ref[:, sl]).T.astype(BF16)


def _qkv_prep(lay, z, cos_l, sin_l):
    tm = lay.tm
    zspec = lambda cb: pl.BlockSpec((tm, DA_W), lambda i: (i, cb))
    tspec = pl.BlockSpec((tm, DA_DV), lambda i: (i % lay.tpb, 0))
    return pl.pallas_call(
        _qkv_prep_kernel,
        grid=(lay.n_tiles,),
        in_specs=[zspec(ZC_Q), zspec(ZC_K), tspec, tspec],
        out_specs=[pl.BlockSpec((tm, DA_W), lambda i: (i, 0)),
                   pl.BlockSpec((None, DA_W, tm), lambda i: (i // lay.tpb, 0, i % lay.tpb))],
        out_shape=[jax.ShapeDtypeStruct((lay.n_all, DA_W), BF16),
                   jax.ShapeDtypeStruct((lay.batch, DA_W, lay.s_all), BF16)],
        compiler_params=_params(("arbitrary",), 32),
        name="qkv_prep",
    )(z, z, cos_l, sin_l)


def _attn_kernel(lam_ref, q_ref, kt_ref, v_ref, g_ref, o_ref, *, ctx_tiles, ctx_len, lam_init):
    lp = lam_ref[...]
    lam = (jnp.exp(jnp.sum(lp[0:1] * lp[1:2], axis=-1, keepdims=True))
           - jnp.exp(jnp.sum(lp[2:3] * lp[3:4], axis=-1, keepdims=True)) + lam_init)

    def attend(n_k):
        for hh in range(ATTN_HEADS_PER_STEP):
            hs = slice(hh * DA_DV, (hh + 1) * DA_DV)
            q = q_ref[:, hs]
            lane = lax.broadcasted_iota(jnp.int32, q.shape, 1)
            zero = jnp.zeros_like(q)
            kt = kt_ref[hs, :n_k]
            v = v_ref[:n_k, hs]

            def softmax_v(qm):
                s = jnp.dot(qm, kt, preferred_element_type=F32)
                p = jnp.exp2(s - jnp.max(s, axis=-1, keepdims=True))
                l = jnp.sum(p, axis=-1, keepdims=True)
                return jnp.dot(p.astype(BF16), v, preferred_element_type=F32) / l

            o = softmax_v(jnp.where(lane < DA_DH, q, zero)) - lam * softmax_v(jnp.where(lane >= DA_DH, q, zero))
            y = o * lax.rsqrt(jnp.mean(o * o, axis=-1, keepdims=True) + LN_EPS)
            o_ref[:, hs] = (y * g_ref[...] * (1.0 - lam_init)).astype(o_ref.dtype)

    is_ctx = pl.program_id(2) < ctx_tiles

    @pl.when(is_ctx)
    def _():
        attend(ctx_len)

    @pl.when(jnp.logical_not(is_ctx))
    def _():
        attend(kt_ref.shape[1])


ATTN_HEADS_PER_STEP = 4


def _attention(lay, q, kt, z, lam_p, norm_g, layer_idx):
    tq = lay.tm
    hw = ATTN_HEADS_PER_STEP * DA_DV
    lam_init = 0.8 - 0.6 * math.exp(-0.3 * layer_idx)
    kern = functools.partial(_attn_kernel, ctx_tiles=lay.ct, ctx_len=lay.ctx, lam_init=lam_init)
    v_cb = ZC_V * (1024 // hw)
    return pl.pallas_call(
        kern,
        grid=(lay.batch, DA_W // hw, lay.tpb),
        in_specs=[pl.BlockSpec((4, DA_DH), lambda b, h, i: (0, 0)),
                  pl.BlockSpec((tq, hw), lambda b, h, i: (b * lay.tpb + i, h)),
                  pl.BlockSpec((None, hw, lay.s_all), lambda b, h, i: (b, h, 0)),
                  pl.BlockSpec((lay.s_all, hw), lambda b, h, i: (b, v_cb + h)),
                  pl.BlockSpec((1, DA_DV), lambda b, h, i: (0, 0))],
        out_specs=pl.BlockSpec((tq, hw), lambda b, h, i: (b * lay.tpb + i, h)),
        out_shape=jax.ShapeDtypeStruct((lay.n_all, DA_W), BF16),
        compiler_params=_params(("arbitrary", "arbitrary", "arbitrary"), 56),
        name="diff_attn",
    )(lam_p, q, kt, z, norm_g.reshape(1, DA_DV))


def _conv_kernel(v_ref, g_ref, vp_ref, gp_ref, vn_ref, gn_ref, dw_ref, db_ref, lg_ref, lb_ref,
                 o_ref, hp_ref, acc_ref, sh_ref, *, tpb, ct):
    tm = v_ref.shape[0]
    ti = pl.program_id(0) % tpb
    has_prev = jnp.logical_and(ti != 0, ti != ct)
    has_next = jnp.logical_and(ti != ct - 1, ti != tpb - 1)
    glu = lambda v, g: v.astype(F32) * jax.nn.sigmoid(g.astype(F32))
    hp_ref[0:CONV_HALO, :] = jnp.where(has_prev, glu(vp_ref[...], gp_ref[...]), 0.0)
    hp_ref[CONV_HALO:CONV_HALO + tm, :] = glu(v_ref[...], g_ref[...])
    hp_ref[CONV_HALO + tm:, :] = jnp.where(has_next, glu(vn_ref[...], gn_ref[...]), 0.0)

    first = CONV_HALO - CONV_K // 2
    rows = tm // 2
    span = sh_ref.shape[1]
    n_copy = 0
    for c in range(CONV_W // V7X_LANES):
        cs = pl.ds(c * V7X_LANES, V7X_LANES)
        for r0 in range(0, tm, rows):
            acc = jnp.zeros((rows, V7X_LANES), F32)
            for res in range(V7X_SUBLANES):
                buf = sh_ref.at[n_copy % 2]
                n_copy += 1
                buf[...] = hp_ref[pl.ds(r0 + res, span), cs]
                win = buf[...]
                for off in range(res, first + CONV_K, V7X_SUBLANES):
                    k = off - first
                    if k >= 0:
                        acc = acc + dw_ref[k:k + 1, cs] * win[off - res:off - res + rows]
            acc_ref[r0:r0 + rows, cs] = acc
    y = _ln(acc_ref[...] + db_ref[...]) * lg_ref[...] + lb_ref[...]
    o_ref[...] = (y * jax.nn.sigmoid(y)).astype(o_ref.dtype)


def _conv_branch(lay, z, dw, db, ln_g, ln_b):
    tm = lay.tm
    hb = tm // CONV_HALO
    n_halo = lay.n_all // CONV_HALO
    main = lambda cb: pl.BlockSpec((tm, CONV_W), lambda i: (i, cb))
    prev = lambda cb: pl.BlockSpec((CONV_HALO, CONV_W), lambda i: (jnp.maximum(i * hb - 1, 0), cb))
    nxt = lambda cb: pl.BlockSpec((CONV_HALO, CONV_W), lambda i: (jnp.minimum((i + 1) * hb, n_halo - 1), cb))
    vec = pl.BlockSpec((1, CONV_W), lambda i: (0, 0))
    kern = functools.partial(_conv_kernel, tpb=lay.tpb, ct=lay.ct)
    return pl.pallas_call(
        kern,
        grid=(lay.n_tiles,),
        in_specs=[main(ZC_CONV_V), main(ZC_CONV_G), prev(ZC_CONV_V), prev(ZC_CONV_G),
                  nxt(ZC_CONV_V), nxt(ZC_CONV_G),
                  pl.BlockSpec((CONV_K, CONV_W), lambda i: (0, 0)), vec, vec, vec],
        out_specs=pl.BlockSpec((tm, CONV_W), lambda i: (i, 0)),
        out_shape=jax.ShapeDtypeStruct((lay.n_all, CONV_W), BF16),
        scratch_shapes=[pltpu.VMEM((tm + 2 * CONV_HALO, CONV_W), F32), pltpu.VMEM((tm, CONV_W), F32),
                        pltpu.VMEM((2, tm // 2 + (CONV_HALO + CONV_K // 2) // V7X_SUBLANES * V7X_SUBLANES,
                                    V7X_LANES), F32)],
        compiler_params=_params(("arbitrary",), 32),
        name="conv_branch",
    )(z, z, z, z, z, z, dw, db.reshape(1, -1), ln_g.reshape(1, -1), ln_b.reshape(1, -1))


LRU_CH = V7X_LANES
LRU_PAD = V7X_SUBLANES


def _gelu_tanh(x):
    return 0.5 * x * (1.0 + jnp.tanh(math.sqrt(2.0 / math.pi) * (x + 0.044715 * (x * x * x))))


def _softplus(x):
    return jnp.maximum(x, 0.0) + jnp.log1p(jnp.exp(-jnp.abs(x)))


def _tile_scan(a, b, reverse):
    row = lax.broadcasted_iota(jnp.int32, a.shape, 1)
    for s in (1, 2, 4):
        if reverse:
            valid = row < V7X_SUBLANES - s
            shift = V7X_SUBLANES - s
        else:
            valid = row >= s
            shift = s
        a_sh = pltpu.roll(a, shift, 1)
        b_sh = pltpu.roll(b, shift, 1)
        b = jnp.where(valid, a * b_sh + b, b)
        a = jnp.where(valid, a * a_sh, a)
    return a, b


def _lru_kernel(zx_ref, zg_ref, cw_ref, cb_ref, gw_ref, gb_ref, lam_ref, o_ref,
                xp_ref, af_ref, bf_ref, ab_ref, bb_ref, *, ctx_len, chunk):
    s_all = zx_ref.shape[0]
    ch = zx_ref.shape[1]
    xp_ref[0:LRU_PAD, :] = jnp.zeros((LRU_PAD, ch), F32)
    xp_ref[LRU_PAD:LRU_PAD + s_all, :] = zx_ref[...].astype(F32)
    xp_ref[LRU_PAD + s_all:, :] = jnp.zeros((LRU_PAD, ch), F32)

    decay = _softplus(-lam_ref[...]) * (-LRU_C * math.log2(math.e))
    sigmoid = lambda x: 0.5 * jnp.tanh(0.5 * x) + 0.5
    gw = gw_ref[...]
    seg_starts = (0, ctx_len)
    seg_ends = (ctx_len, s_all)
    for r0 in range(0, s_all, chunk):
        row = r0 + lax.broadcasted_iota(jnp.int32, (chunk, ch), 0)
        xs = jnp.zeros((chunk, ch), F32) + cb_ref[...]
        for k in range(LRU_CONV_K):
            off = k - 2
            x = xp_ref[pl.ds(LRU_PAD + r0 + off, chunk), :]
            if off < 0 and r0 in seg_starts:
                x = jnp.where(row - r0 < -off, 0.0, x)
            if off > 0 and r0 + chunk in seg_ends:
                x = jnp.where(row - r0 >= chunk - off, 0.0, x)
            xs = xs + cw_ref[k:k + 1, :] * x
        pre = jnp.dot(xs.astype(BF16), gw, preferred_element_type=F32) + gb_ref[...]
        for d, (a_ref, b_ref) in enumerate(((af_ref, bf_ref), (ab_ref, bb_ref))):
            r = sigmoid(pre[:, (2 * d) * ch:(2 * d + 1) * ch])
            i = sigmoid(pre[:, (2 * d + 1) * ch:(2 * d + 2) * ch])
            a = jnp.exp2(r * decay[d:d + 1, :])
            b = jnp.sqrt(1.0 - a * a) * (i * xs)
            a_t, b_t = _tile_scan(a.reshape(chunk // V7X_SUBLANES, V7X_SUBLANES, ch),
                                  b.reshape(chunk // V7X_SUBLANES, V7X_SUBLANES, ch), reverse=(d == 1))
            a_ref[pl.ds(r0, chunk), :] = a_t.reshape(chunk, ch)
            b_ref[pl.ds(r0, chunk), :] = b_t.reshape(chunk, ch)

    n_t = s_all // V7X_SUBLANES
    c_t = ctx_len // V7X_SUBLANES

    def fwd_step(t, carry):
        rows = pl.ds(pl.multiple_of(t * V7X_SUBLANES, V7X_SUBLANES), V7X_SUBLANES)
        h = bf_ref[rows, :] + af_ref[rows, :] * carry
        bf_ref[rows, :] = h
        return jnp.broadcast_to(h[V7X_SUBLANES - 1:, :], h.shape)

    def bwd_step(t, carry, hi):
        t = hi - 1 - t
        rows = pl.ds(pl.multiple_of(t * V7X_SUBLANES, V7X_SUBLANES), V7X_SUBLANES)
        h = bb_ref[rows, :] + ab_ref[rows, :] * carry
        bb_ref[rows, :] = h
        return jnp.broadcast_to(h[0:1, :], h.shape)

    zero = jnp.zeros((V7X_SUBLANES, ch), F32)
    lax.fori_loop(0, n_t, fwd_step, zero)
    carry = lax.fori_loop(0, c_t, functools.partial(bwd_step, hi=c_t), zero)
    lax.fori_loop(0, n_t - c_t, functools.partial(bwd_step, hi=n_t), carry)

    for r0 in range(0, s_all, chunk):
        rows = pl.ds(r0, chunk)
        hsum = bf_ref[rows, :] + bb_ref[rows, :]
        o_ref[rows, :] = (hsum * _gelu_tanh(zg_ref[rows, :].astype(F32))).astype(o_ref.dtype)


def _lru_gate_matrix(gate_w):
    n_grp = LRU_W // LRU_CH
    per = LRU_CH // LRU_BS
    w = gate_w.reshape(4, n_grp, per, LRU_BS, LRU_BS)
    eye = jnp.eye(per, dtype=gate_w.dtype)
    full = jnp.einsum("gcpjk,pq->cpjgqk", w, eye)
    return full.reshape(n_grp, LRU_CH, 4 * LRU_CH)


def _lru_branch(lay, z, conv_w, conv_b, gate_w, gate_b, lam):
    s_all = lay.s_all
    n_grp = LRU_W // LRU_CH
    chunk = lay.tm
    gw = _lru_gate_matrix(gate_w).astype(BF16)
    gb = gate_b.reshape(4, n_grp, LRU_CH).transpose(1, 0, 2).reshape(n_grp, 1, 4 * LRU_CH)
    kern = functools.partial(_lru_kernel, ctx_len=lay.ctx, chunk=chunk)
    cb_x = ZC_LRU_X * (1024 // LRU_CH)
    cb_g = ZC_LRU_G * (1024 // LRU_CH)
    return pl.pallas_call(
        kern,
        grid=(lay.batch, n_grp),
        in_specs=[pl.BlockSpec((s_all, LRU_CH), lambda b, c: (b, cb_x + c)),
                  pl.BlockSpec((s_all, LRU_CH), lambda b, c: (b, cb_g + c)),
                  pl.BlockSpec((LRU_CONV_K, LRU_CH), lambda b, c: (0, c)),
                  pl.BlockSpec((1, LRU_CH), lambda b, c: (0, c)),
                  pl.BlockSpec((None, LRU_CH, 4 * LRU_CH), lambda b, c: (c, 0, 0)),
                  pl.BlockSpec((None, 1, 4 * LRU_CH), lambda b, c: (c, 0, 0)),
                  pl.BlockSpec((2, LRU_CH), lambda b, c: (0, c))],
        out_specs=pl.BlockSpec((s_all, LRU_CH), lambda b, c: (b, c)),
        out_shape=jax.ShapeDtypeStruct((lay.n_all, LRU_W), BF16),
        scratch_shapes=[pltpu.VMEM((s_all + 2 * LRU_PAD, LRU_CH), F32)]
        + [pltpu.VMEM((s_all, LRU_CH), F32) for _ in range(4)],
        compiler_params=_params(("arbitrary", "arbitrary"), 48),
        name="rglru",
    )(z, z, conv_w, conv_b.reshape(1, -1), gw, gb, lam)


def _merge_kernel(hc_ref, ha_ref, hl_ref, g0a, g0b, g1a, g1b, g2a, g2b, wc_ref, wa_ref, wl_ref, o_ref):
    half = D_MODEL // 2
    branches = ((hc_ref, wc_ref, (g0a, g0b)), (ha_ref, wa_ref, (g1a, g1b)), (hl_ref, wl_ref, (g2a, g2b)))
    for half_idx in range(2):
        cs = slice(half_idx * half, (half_idx + 1) * half)
        acc = None
        for h_ref, w_ref, gates in branches:
            p = (jax.nn.sigmoid(gates[half_idx][...].astype(F32))
                 * jnp.dot(h_ref[...], w_ref[:, cs], preferred_element_type=F32))
            acc = p if acc is None else acc + p
        o_ref[:, cs] = acc.astype(o_ref.dtype)


def _merge(lay, z, hc, ha, hl, wc, wa, wl, layer):
    tm = lay.tm
    hspec = pl.BlockSpec((tm, 1024), lambda i: (i, 0))
    gspec = lambda cb: pl.BlockSpec((tm, 1024), lambda i: (i, ZC_MERGE + cb))
    wspec = pl.BlockSpec((None, 1024, D_MODEL), lambda i: (layer, 0, 0))
    return pl.pallas_call(
        _merge_kernel,
        grid=(lay.n_tiles,),
        in_specs=[hspec, hspec, hspec] + [gspec(cb) for cb in range(6)] + [wspec, wspec, wspec],
        out_specs=pl.BlockSpec((tm, D_MODEL), lambda i: (i, 0)),
        out_shape=jax.ShapeDtypeStruct((lay.n_all, D_MODEL), BF16),
        compiler_params=_params(("arbitrary",), 56),
        name="merge",
    )(hc, ha, hl, z, z, z, z, z, z, wc, wa, wl)


def _mix_kernel(m_ref, h_ref, wo_ref, g1_ref, lg_ref, lb_ref, sh_ref, sc_ref, rw_ref, rb_ref,
                h_out, u_out, idx_out, wgt_out, rank_out, cnt_out, run_ref):
    @pl.when(pl.program_id(0) == 0)
    def _():
        run_ref[...] = jnp.zeros_like(run_ref)

    mix = jnp.dot(m_ref[...], wo_ref[...], preferred_element_type=F32)
    x = _ln(DEEPNORM_ALPHA * h_ref[...] + g1_ref[...] * mix) * lg_ref[...] + lb_ref[...]
    h_out[...] = x
    u = _ln(x) * (1.0 + sc_ref[...]) + sh_ref[...]
    u_hi = u.astype(BF16)
    u_out[...] = u_hi

    u_lo = (u - u_hi.astype(F32)).astype(BF16)
    hi_all = jnp.dot(u_hi, rw_ref[...], preferred_element_type=F32)
    logits = (hi_all[:, :V7X_LANES] + hi_all[:, V7X_LANES:]
              + jnp.dot(u_lo, rw_ref[:, :V7X_LANES], preferred_element_type=F32) + rb_ref[...])
    tm = logits.shape[0]
    out_lane = lax.broadcasted_iota(jnp.int32, logits.shape, 1)
    lane = out_lane.astype(F32)
    idx_acc = jnp.zeros(logits.shape, F32)
    val_acc = jnp.zeros(logits.shape, F32)
    chosen = jnp.zeros(logits.shape, F32)
    hits = []
    top = None
    denom = None
    for k in range(TOP_K):
        mx = jnp.max(logits, axis=-1, keepdims=True)
        idx = jnp.min(jnp.where(logits == mx, lane, float(V7X_LANES)), axis=-1, keepdims=True)
        hit = lane == idx
        hits.append(hit)
        chosen = jnp.where(hit, 1.0, chosen)
        logits = jnp.where(hit, -jnp.inf, logits)
        if k == 0:
            top = mx
        e = jnp.exp(mx - top)
        denom = e if k == 0 else denom + e
        idx_acc = jnp.where(out_lane == k, idx, idx_acc)
        val_acc = jnp.where(out_lane == k, e, val_acc)
    idx_out[...] = idx_acc.astype(jnp.int32)
    wgt_out[...] = val_acc / denom

    earlier = (lax.broadcasted_iota(jnp.int32, (tm, tm), 0) > lax.broadcasted_iota(jnp.int32, (tm, tm), 1))
    before = jnp.dot(jnp.where(earlier, 1.0, 0.0).astype(BF16), chosen.astype(BF16),
                     preferred_element_type=F32) + run_ref[...]
    rank_acc = jnp.zeros(logits.shape, F32)
    for k in range(TOP_K):
        r = jnp.sum(jnp.where(hits[k], before, 0.0), axis=-1, keepdims=True)
        rank_acc = jnp.where(out_lane == k, r, rank_acc)
    rank_out[...] = rank_acc.astype(jnp.int32)
    run_ref[...] = run_ref[...] + jnp.sum(chosen, axis=0, keepdims=True)
    cnt_out[...] = jnp.broadcast_to(run_ref[...], cnt_out.shape)


def _mix(lay, m, h, wo, layer, g1, ln_g, ln_b, sh2, sc2, router_w, router_b):
    tm = lay.tm
    d = D_MODEL
    row = pl.BlockSpec((tm, d), lambda i: (i, 0))
    mod = pl.BlockSpec((None, 1, d), lambda i: (lay.mod_row(i), 0, 0))
    vec = pl.BlockSpec((1, d), lambda i: (0, 0))
    small = pl.BlockSpec((tm, V7X_LANES), lambda i: (i, 0))
    pad = V7X_LANES - N_EXPERTS
    rw = jnp.pad(router_w, ((0, 0), (0, pad)))
    rw_hi = rw.astype(BF16)
    rw = jnp.concatenate([rw_hi, (rw - rw_hi.astype(F32)).astype(BF16)], axis=1)
    rb = jnp.pad(router_b, (0, pad), constant_values=-jnp.inf).reshape(1, V7X_LANES)
    return pl.pallas_call(
        _mix_kernel,
        grid=(lay.n_tiles,),
        in_specs=[row, row, pl.BlockSpec((None, d, d), lambda i: (layer, 0, 0)), mod, vec, vec, mod, mod,
                  pl.BlockSpec((d, 2 * V7X_LANES), lambda i: (0, 0)),
                  pl.BlockSpec((1, V7X_LANES), lambda i: (0, 0))],
        out_specs=[row, row, small, small, small, pl.BlockSpec((V7X_SUBLANES, V7X_LANES), lambda i: (0, 0))],
        out_shape=[jax.ShapeDtypeStruct((lay.n_all, d), F32),
                   jax.ShapeDtypeStruct((lay.n_all, d), BF16),
                   jax.ShapeDtypeStruct((lay.n_all, V7X_LANES), jnp.int32),
                   jax.ShapeDtypeStruct((lay.n_all, V7X_LANES), F32),
                   jax.ShapeDtypeStruct((lay.n_all, V7X_LANES), jnp.int32),
                   jax.ShapeDtypeStruct((V7X_SUBLANES, V7X_LANES), F32)],
        scratch_shapes=[pltpu.VMEM((1, V7X_LANES), F32)],
        compiler_params=_params(("arbitrary",), 56),
        name="mix_ln1_router",
    )(m, h, wo, g1, ln_g.reshape(1, d), ln_b.reshape(1, d), sh2, sc2, rw, rb)


W1_PERM = 256


def _deinterleave_matrix():
    half = W1_PERM // 2
    src = jnp.arange(W1_PERM, dtype=jnp.int32)[:, None]
    dst = jnp.arange(W1_PERM, dtype=jnp.int32)[None, :]
    want = jnp.where(dst < half, 2 * dst, 2 * (dst - half) + 1)
    return (src == want).astype(BF16)


def _moe_kernel(blk_e_ref, first_ref, next_ref, n_used_ref,
                x_ref, p_ref, b1g_ref, b1l_ref, b2_ref, w1_hbm, w2_hbm,
                o_ref, st1_ref, st2_ref, w1g_ref, w1l_ref, w2bf_ref, sem, *, e0):
    i = pl.program_id(0)
    used = i < n_used_ref[0]

    def weight_copies(e):
        return (pltpu.make_async_copy(w1_hbm.at[e], st1_ref, sem.at[0]),
                pltpu.make_async_copy(w2_hbm.at[e], st2_ref, sem.at[1]))

    @pl.when(jnp.logical_and(used, first_ref[i] == 1))
    def _():
        @pl.when(i == 0)
        def _():
            for cp in weight_copies(e0 + blk_e_ref[0]):
                cp.start()

        for cp in weight_copies(e0 + blk_e_ref[i]):
            cp.wait()

        half = W1_PERM // 2

        def slab(r, carry):
            rows = pl.ds(pl.multiple_of(r * W1_PERM, W1_PERM), W1_PERM)
            for j in range(st1_ref.shape[1] // W1_PERM):
                w = st1_ref[rows, j * W1_PERM:(j + 1) * W1_PERM].astype(BF16)
                y = jnp.dot(w, p_ref[...], preferred_element_type=F32)
                w1g_ref[rows, j * half:(j + 1) * half] = y[:, :half].astype(BF16)
                w1l_ref[rows, j * half:(j + 1) * half] = y[:, half:].astype(BF16)
            return carry

        lax.fori_loop(0, st1_ref.shape[0] // W1_PERM, slab, 0)
        w2bf_ref[...] = st2_ref[...].astype(BF16)

        @pl.when(next_ref[i] >= 0)
        def _():
            for cp in weight_copies(e0 + next_ref[i]):
                cp.start()

    @pl.when(used)
    def _():
        x = x_ref[...]
        gl = jnp.minimum(jnp.dot(x, w1g_ref[...], preferred_element_type=F32) + b1g_ref[...], SWIGLU_LIMIT)
        lin = jnp.clip(jnp.dot(x, w1l_ref[...], preferred_element_type=F32) + b1l_ref[...],
                       -SWIGLU_LIMIT, SWIGLU_LIMIT)
        act = gl * jax.nn.sigmoid(SWIGLU_ALPHA * gl) * (lin + 1.0)
        y = jnp.dot(act.astype(BF16), w2bf_ref[...], preferred_element_type=F32) + b2_ref[...]
        o_ref[...] = y.astype(o_ref.dtype)

    @pl.when(jnp.logical_not(used))
    def _():
        o_ref[...] = jnp.zeros_like(o_ref)


def _moe_ffn_blocks(xs, blk_e, blk_first, blk_next, n_used, w1, w2, b1g, b1l, b2, layer):
    n_slots, d = xs.shape
    n_blocks = n_slots // MOE_ROWS
    f = w2.shape[1]
    e0 = layer * N_EXPERTS
    bsel = lambda width: pl.BlockSpec((None, 1, width), lambda i, be, bf, bn, nu: (e0 + be[i], 0, 0))
    grid_spec = pltpu.PrefetchScalarGridSpec(
        num_scalar_prefetch=4,
        grid=(n_blocks,),
        in_specs=[pl.BlockSpec((MOE_ROWS, d), lambda i, be, bf, bn, nu: (i, 0)),
                  pl.BlockSpec((W1_PERM, W1_PERM), lambda i, be, bf, bn, nu: (0, 0)),
                  bsel(f), bsel(f), bsel(d),
                  pl.BlockSpec(memory_space=pl.ANY), pl.BlockSpec(memory_space=pl.ANY)],
        out_specs=pl.BlockSpec((MOE_ROWS, d), lambda i, be, bf, bn, nu: (i, 0)),
        scratch_shapes=[pltpu.VMEM((d, 2 * f), F32), pltpu.VMEM((f, d), F32),
                        pltpu.VMEM((d, f), BF16), pltpu.VMEM((d, f), BF16), pltpu.VMEM((f, d), BF16),
                        pltpu.SemaphoreType.DMA((2,))],
    )
    return pl.pallas_call(
        functools.partial(_moe_kernel, e0=e0),
        grid_spec=grid_spec,
        out_shape=jax.ShapeDtypeStruct((n_slots, d), BF16),
        compiler_params=_params(("arbitrary",), 58),
        name="moe_ffn",
    )(blk_e, blk_first, blk_next, n_used, xs, _deinterleave_matrix(), b1g, b1l, b2, w1, w2)


def _moe_dispatch(top_idx, rank, counts, n_tok):
    n_assign = n_tok * TOP_K
    n_blocks = -(-n_assign // MOE_ROWS) + N_EXPERTS
    n_slots = n_blocks * MOE_ROWS
    padded = (counts + MOE_ROWS - 1) // MOE_ROWS * MOE_ROWS
    pad_end = jnp.cumsum(padded)
    pad_start = pad_end - padded
    slot_of = pad_start.at[top_idx].get(mode="promise_in_bounds") + rank
    tok = jnp.broadcast_to(jnp.arange(n_tok, dtype=jnp.int32)[:, None], (n_tok, TOP_K))
    marked = jnp.zeros((n_slots,), jnp.int32).at[slot_of.reshape(-1)].add(tok.reshape(-1) + 1)
    slot_tok = jnp.where(marked > 0, marked - 1, jnp.arange(n_slots, dtype=jnp.int32) % n_tok)
    blk_idx = jnp.arange(n_blocks, dtype=jnp.int32)
    blk_e = jnp.minimum(jnp.sum((blk_idx * MOE_ROWS)[:, None] >= pad_end[None, :], axis=1),
                        N_EXPERTS - 1).astype(jnp.int32)
    blk_first = jnp.concatenate([jnp.ones((1,), jnp.int32), (blk_e[1:] != blk_e[:-1]).astype(jnp.int32)])
    n_used = (pad_end[-1] // MOE_ROWS).astype(jnp.int32)
    starts = jnp.where(jnp.logical_and(blk_first == 1, blk_idx < n_used), blk_idx, n_blocks)
    nxt = jnp.concatenate([lax.cummin(starts, axis=0, reverse=True)[1:], jnp.full((1,), n_blocks, jnp.int32)])
    blk_next = jnp.where(nxt < n_blocks, blk_e.at[jnp.minimum(nxt, n_blocks - 1)].get(mode="promise_in_bounds"),
                         -1).astype(jnp.int32)
    return slot_tok, slot_of, blk_e, blk_first, blk_next, n_used.reshape(1)


def _final_kernel(h_ref, y_ref, w_ref, g2_ref, lg_ref, lb_ref, *rest, with_next):
    if with_next:
        sh_ref, sc_ref, h_out, u_out = rest
    else:
        (h_out,) = rest
    w = w_ref[...]
    y = w[:, 0:1] * y_ref[0].astype(F32)
    for k in range(1, TOP_K):
        y = y + w[:, k:k + 1] * y_ref[k].astype(F32)
    x = _ln(DEEPNORM_ALPHA * h_ref[...] + g2_ref[...] * y) * lg_ref[...] + lb_ref[...]
    h_out[...] = x
    if with_next:
        u_out[...] = (_ln(x) * (1.0 + sc_ref[...]) + sh_ref[...]).astype(u_out.dtype)


def _final(lay, h, yk, top_w, g2, ln_g, ln_b, next_mod):
    tm = lay.tm
    d = D_MODEL
    row = pl.BlockSpec((tm, d), lambda i: (i, 0))
    mod = pl.BlockSpec((None, 1, d), lambda i: (lay.mod_row(i), 0, 0))
    vec = pl.BlockSpec((1, d), lambda i: (0, 0))
    in_specs = [row, pl.BlockSpec((TOP_K, tm, d), lambda i: (0, i, 0)),
                pl.BlockSpec((tm, V7X_LANES), lambda i: (i, 0)), mod, vec, vec]
    args = [h, yk, top_w, g2, ln_g.reshape(1, d), ln_b.reshape(1, d)]
    out_specs = [row]
    out_shape = [jax.ShapeDtypeStruct((lay.n_all, d), F32)]
    if next_mod is not None:
        in_specs += [mod, mod]
        args += list(next_mod)
        out_specs.append(row)
        out_shape.append(jax.ShapeDtypeStruct((lay.n_all, d), BF16))
    outs = pl.pallas_call(
        functools.partial(_final_kernel, with_next=next_mod is not None),
        grid=(lay.n_tiles,),
        in_specs=in_specs,
        out_specs=out_specs,
        out_shape=out_shape,
        compiler_params=_params(("arbitrary",), 48),
        name="combine_ln2",
    )(*args)
    return outs if next_mod is not None else (outs[0], None)


def kernel(x, c, ctx, c_ctx, w_ada, b_ada, w_in, b_in, conv_dw, conv_db, conv_ln_g, conv_ln_b, w_conv_out,
           da_lambda, da_norm_g, w_da_out, lru_conv_w, lru_conv_b, lru_gate_w, lru_gate_b, lru_lambda,
           w_lru_out, w_out, ln1_g, ln1_b, router_w, router_b, moe_w1, moe_b1, moe_w2, moe_b2, ln2_g, ln2_b):
    batch, seq, d = x.shape
    lay = _Layout(batch, seq, ctx.shape[1])
    h = jnp.concatenate([ctx, x], axis=1).reshape(lay.n_all, d)

    c_rows = jnp.zeros((16, d), F32).at[:batch].set(c).at[batch].set(c_ctx)
    mods = []
    for l in range(DEPTH):
        ada = _ada(c_rows, w_ada, b_ada, l)[:batch + 1]
        mods.append([m.reshape(batch + 1, 1, d) for m in jnp.split(ada, 6, axis=-1)])

    n_exp, d_in, f2 = moe_w1.shape[1:]
    w1_all = moe_w1.reshape(DEPTH * n_exp, d_in, f2)
    w2_all = moe_w2.reshape(DEPTH * n_exp, f2 // 2, d)
    b1_all = moe_b1.reshape(DEPTH * n_exp, 1, f2)
    b1g_all, b1l_all = b1_all[:, :, 0::2], b1_all[:, :, 1::2]
    b2_all = moe_b2.reshape(DEPTH * n_exp, 1, d)
    wc_bf, wa_bf, wl_bf, wo_bf = (w.astype(BF16) for w in (w_conv_out, w_da_out, w_lru_out, w_out))

    cos_l, sin_l = _rope_tables(lay)
    u = _ln_mod(lay, h, mods[0][0], mods[0][1])
    for l in range(DEPTH):
        sh1, sc1, g1, sh2, sc2, g2 = mods[l]
        tm_in = next((t for t in (1024, 512) if lay.n_all % t == 0), lay.tm)
        z = _mm(u, w_in, b_in, l, tm_in, 1024, BF16)

        hc = _conv_branch(lay, z, conv_dw[l], conv_db[l], conv_ln_g[l], conv_ln_b[l])
        q, kt = _qkv_prep(lay, z, cos_l, sin_l)
        ha = _attention(lay, q, kt, z, da_lambda[l], da_norm_g[l], l)
        hl = _lru_branch(lay, z, lru_conv_w[l], lru_conv_b[l], lru_gate_w[l], lru_gate_b[l], lru_lambda[l])
        m = _merge(lay, z, hc, ha, hl, wc_bf, wa_bf, wl_bf, l)
        h, u2, top_idx, top_w, rank, counts = _mix(lay, m, h, wo_bf, l, g1, ln1_g[l], ln1_b[l],
                                                   sh2, sc2, router_w[l], router_b[l])

        slot_tok, slot_of, blk_e, blk_first, blk_next, n_used = _moe_dispatch(
            top_idx[:, :TOP_K], rank[:, :TOP_K], counts[0, :N_EXPERTS].astype(jnp.int32), lay.n_all)
        xs = u2.at[slot_tok].get(mode="promise_in_bounds")
        outs = _moe_ffn_blocks(xs, blk_e, blk_first, blk_next, n_used, w1_all, w2_all,
                               b1g_all, b1l_all, b2_all, l)
        yk = outs.at[slot_of.T].get(mode="promise_in_bounds")
        next_mod = (mods[l + 1][0], mods[l + 1][1]) if l + 1 < DEPTH else None
        h, u = _final(lay, h, yk, top_w, g2, ln2_g[l], ln2_b[l], next_mod)

    return h.reshape(batch, lay.s_all, d)[:, lay.ctx:, :]
```
